```python
import jax
import jax.numpy as jnp
from jax import lax
import numpy as np

D_MODEL = 2048
BATCH = 4
SEQ = 8192
DEPTH = 1

GRID_W = 64
CTX_LEN = 256
ATTN_HEADS = 16
ATTN_KV_HEADS = 4
HEAD_DIM = 64
GQA_GROUP = ATTN_HEADS // ATTN_KV_HEADS
WINDOW = 128
ATTN_BLOCK = 128
ROPE_BASE = 10000.0
GLA_HEADS = 4
GLA_DK = 128
GLA_DV = 256
GLA_GATE_RANK = 16
GLA_GATE_TEMP = 16.0
GLA_CHUNK = 64
ATTN_WIDTH = ATTN_HEADS * HEAD_DIM
GLA_WIDTH = GLA_HEADS * GLA_DV
MIX_WIDTH = ATTN_WIDTH + GLA_WIDTH
IN_SPLITS = (ATTN_WIDTH, ATTN_KV_HEADS * HEAD_DIM, ATTN_KV_HEADS * HEAD_DIM,
             GLA_HEADS * GLA_DK, GLA_HEADS * GLA_DK, GLA_WIDTH, GLA_WIDTH, 2 * GLA_GATE_RANK)
IN_WIDTH = sum(IN_SPLITS)
N_EXPERTS = 32
TOP_K = 4
D_EXPERT = D_MODEL
SWIGLU_LIMIT = 7.0
SWIGLU_ALPHA = 1.702
MOE_BLOCK = 256
NORM_EPS = 1e-6

kernel_name = 'hybrid_swa_gla_moe_dit_layer'


def rms_norm(x, g):
    xf = x.astype(jnp.float32)
    y = xf * lax.rsqrt(jnp.mean(xf * xf, axis=-1, keepdims=True) + NORM_EPS)
    return (y * g.astype(jnp.float32)).astype(x.dtype)


def modulate(h, shift, scale):
    return h * (1 + scale) + shift


def axial_rope_angles(n):
    rows = n // GRID_W
    row = jnp.repeat(jnp.arange(rows, dtype=jnp.float32), GRID_W)
    col = jnp.tile(jnp.arange(GRID_W, dtype=jnp.float32), rows)
    n_freq = HEAD_DIM // 4
    inv_freq = ROPE_BASE ** (-jnp.arange(n_freq, dtype=jnp.float32) / n_freq)
    ang = jnp.stack([row[:, None] * inv_freq, col[:, None] * inv_freq], axis=1)
    return jnp.cos(ang), jnp.sin(ang)


def apply_axial_rope(x, cos, sin):
    b, n, h, d = x.shape
    xf = x.astype(jnp.float32).reshape(b, n, h, 2, 2, d // 4)
    x1, x2 = xf[..., 0, :], xf[..., 1, :]
    c, s = cos[None, :, None], sin[None, :, None]
    out = jnp.stack([x1 * c - x2 * s, x1 * s + x2 * c], axis=-2)
    return out.reshape(b, n, h, d).astype(x.dtype)


def split_projection(p):
    b, l, _ = p.shape
    q_a, k_a, v_a, q_g, k_g, v_g, r_g, z_g = jnp.split(p, np.cumsum(IN_SPLITS)[:-1].tolist(), axis=-1)
    return (q_a.reshape(b, l, ATTN_HEADS, HEAD_DIM),
            k_a.reshape(b, l, ATTN_KV_HEADS, HEAD_DIM),
            v_a.reshape(b, l, ATTN_KV_HEADS, HEAD_DIM),
            q_g.reshape(b, l, GLA_HEADS, GLA_DK),
            k_g.reshape(b, l, GLA_HEADS, GLA_DK),
            v_g.reshape(b, l, GLA_HEADS, GLA_DV),
            r_g,
            z_g.reshape(b, l, 2, GLA_GATE_RANK))


def window_attention(q, k, v, k_ctx, v_ctx, sink):
    b, n = q.shape[:2]
    m = k_ctx.shape[1]
    n_blocks = n // ATTN_BLOCK
    band = ATTN_BLOCK + 2 * WINDOW
    qg = (q * HEAD_DIM ** -0.5).reshape(b, n, ATTN_KV_HEADS, GQA_GROUP, HEAD_DIM)
    pad = ((0, 0), (WINDOW, WINDOW), (0, 0), (0, 0))
    k_pad = jnp.pad(k, pad)
    v_pad = jnp.pad(v, pad)
    offset = jnp.arange(band)[None, :] - jnp.arange(ATTN_BLOCK)[:, None]
    in_window = (offset >= 0) & (offset <= 2 * WINDOW)
    sink_col = jnp.broadcast_to(sink.astype(jnp.float32).reshape(1, ATTN_KV_HEADS, GQA_GROUP, 1, 1),
                                (b, ATTN_KV_HEADS, GQA_GROUP, ATTN_BLOCK, 1))

    def block(bi):
        start = bi * ATTN_BLOCK
        qb = lax.dynamic_slice_in_dim(qg, start, ATTN_BLOCK, axis=1)
        kb = lax.dynamic_slice_in_dim(k_pad, start, band, axis=1)
        vb = lax.dynamic_slice_in_dim(v_pad, start, band, axis=1)
        key_pos = start - WINDOW + jnp.arange(band)
        valid = in_window & ((key_pos >= 0) & (key_pos < n))[None, :]
        s_loc = jnp.einsum('bqkgd,bskd->bkgqs', qb, kb).astype(jnp.float32)
        s_loc = jnp.where(valid, s_loc, -jnp.inf)
        s_ctx = jnp.einsum('bqkgd,bskd->bkgqs', qb, k_ctx).astype(jnp.float32)
        p = jax.nn.softmax(jnp.concatenate([s_loc, s_ctx, sink_col], axis=-1), axis=-1).astype(v.dtype)
        o = (jnp.einsum('bkgqs,bskd->bqkgd', p[..., :band], vb)
             + jnp.einsum('bkgqs,bskd->bqkgd', p[..., band:band + m], v_ctx))
        return o.reshape(b, ATTN_BLOCK, ATTN_WIDTH)

    out = lax.map(block, jnp.arange(n_blocks))
    return jnp.moveaxis(out, 0, 1).reshape(b, n, ATTN_WIDTH)


def context_attention(q_ctx, k_ctx, v_ctx, sink):
    b, m = q_ctx.shape[:2]
    qg = (q_ctx * HEAD_DIM ** -0.5).reshape(b, m, ATTN_KV_HEADS, GQA_GROUP, HEAD_DIM)
    s = jnp.einsum('bqkgd,bskd->bkgqs', qg, k_ctx).astype(jnp.float32)
    sink_col = jnp.broadcast_to(sink.astype(jnp.float32).reshape(1, ATTN_KV_HEADS, GQA_GROUP, 1, 1),
                                (b, ATTN_KV_HEADS, GQA_GROUP, m, 1))
    p = jax.nn.softmax(jnp.concatenate([s, sink_col], axis=-1), axis=-1)[..., :m].astype(v_ctx.dtype)
    o = jnp.einsum('bkgqs,bskd->bqkgd', p, v_ctx)
    return o.reshape(b, m, ATTN_WIDTH)


def gla_log_gate(z, w_a2, b_a2):
    b, l = z.shape[:2]
    g = jax.nn.log_sigmoid((z @ w_a2 + b_a2).astype(jnp.float32)) / GLA_GATE_TEMP
    return g.reshape(b, l, GLA_HEADS, GLA_DK)


def gla_scan(q, k, v, log_a, state0):
    b, l, h, _ = q.shape
    nc = l // GLA_CHUNK

    def to_chunks(t):
        return jnp.moveaxis(t.astype(jnp.float32).reshape(b, nc, GLA_CHUNK, h, t.shape[-1]), 1, 0)

    causal = jnp.tril(jnp.ones((GLA_CHUNK, GLA_CHUNK), dtype=bool))[None, :, :, None, None]

    def step(s, inp):
        qc, kc, vc, gc = inp
        cum = jnp.cumsum(gc, axis=1)
        o_inter = jnp.einsum('bchk,bhkv->bchv', qc * jnp.exp(cum), s)
        diff = cum[:, :, None] - cum[:, None, :]
        decay = jnp.exp(jnp.where(causal, diff, -jnp.inf))
        att = jnp.einsum('bihk,bjhk,bijhk->bhij', qc, kc, decay)
        o_intra = jnp.einsum('bhij,bjhv->bihv', att, vc)
        last = cum[:, -1]
        k_dec = kc * jnp.exp(last[:, None] - cum)
        s_new = jnp.exp(last)[..., None] * s + jnp.einsum('bchk,bchv->bhkv', k_dec, vc)
        return s_new, o_inter + o_intra

    q_scaled = q * GLA_DK ** -0.5
    s_final, o = lax.scan(step, state0, (to_chunks(q_scaled), to_chunks(k), to_chunks(v), to_chunks(log_a)))
    return jnp.moveaxis(o, 0, 1).reshape(b, l, h, v.shape[-1]), s_final


def gla_bidirectional(q, k, v, z, qc, kc, vc, zc, w_a2, b_a2):
    b = q.shape[0]
    o_lat = 0.0
    o_ctx = 0.0
    for d in range(2):
        la = gla_log_gate(z[:, :, d], w_a2[d], b_a2[d])
        lac = gla_log_gate(zc[:, :, d], w_a2[d], b_a2[d])
        lat = (q, k, v, la)
        cx = (qc, kc, vc, lac)
        if d == 1:
            lat = tuple(jnp.flip(t, axis=1) for t in lat)
            cx = tuple(jnp.flip(t, axis=1) for t in cx)
        s0 = jnp.zeros((b, GLA_HEADS, GLA_DK, GLA_DV), jnp.float32)
        oc, s_ctx = gla_scan(*cx, s0)
        ol, _ = gla_scan(*lat, s_ctx)
        if d == 1:
            ol = jnp.flip(ol, axis=1)
            oc = jnp.flip(oc, axis=1)
        o_lat = o_lat + ol
        o_ctx = o_ctx + oc
    return o_lat, o_ctx


def gla_output(o, r, g_gla):
    b, l = o.shape[:2]
    return (rms_norm(o, g_gla).reshape(b, l, GLA_WIDTH) * jax.nn.silu(r.astype(jnp.float32))).astype(r.dtype)


def moe_ffn(h, w_router, b_router, w_gu, b_gu, w_down, b_down):
    t, d = h.shape
    logits = (h @ w_router + b_router).astype(jnp.float32)
    top_logit, top_idx = lax.top_k(logits, TOP_K)
    gates = jax.nn.softmax(top_logit, axis=-1).astype(h.dtype)
    n_assign = t * TOP_K
    expert = top_idx.reshape(n_assign)
    token = jnp.repeat(jnp.arange(t, dtype=jnp.int32), TOP_K)
    gate = gates.reshape(n_assign)
    order = jnp.argsort(expert)
    expert_s, token_s, gate_s = expert[order], token[order], gate[order]
    counts = jnp.bincount(expert, length=N_EXPERTS)
    start = jnp.cumsum(counts) - counts
    padded = (counts + MOE_BLOCK - 1) // MOE_BLOCK * MOE_BLOCK
    pad_end = jnp.cumsum(padded)
    pad_start = pad_end - padded
    dest = pad_start[expert_s] + jnp.arange(n_assign) - start[expert_s]
    n_blocks = -(-(n_assign + N_EXPERTS * (MOE_BLOCK - 1)) // MOE_BLOCK)
    cap = n_blocks * MOE_BLOCK
    buf_token = jnp.zeros((cap,), jnp.int32).at[dest].set(token_s)
    buf_gate = jnp.zeros((cap,), h.dtype).at[dest].set(gate_s)
    block_expert = jnp.minimum(jnp.searchsorted(pad_end, jnp.arange(n_blocks) * MOE_BLOCK, side='right'),
                               N_EXPERTS - 1)

    def expert_block(acc, blk):
        tok, g, e = blk
        xb = h[tok]
        gu = xb @ w_gu[e] + b_gu[e]
        glu = jnp.minimum(gu[:, :D_EXPERT], SWIGLU_LIMIT)
        lin = jnp.clip(gu[:, D_EXPERT:], -SWIGLU_LIMIT, SWIGLU_LIMIT)
        act = glu * jax.nn.sigmoid(SWIGLU_ALPHA * glu) * (lin + 1)
        y = act @ w_down[e] + b_down[e]
        return acc.at[tok].add(y * g[:, None]), None

    out, _ = lax.scan(expert_block, jnp.zeros_like(h),
                      (buf_token.reshape(n_blocks, MOE_BLOCK), buf_gate.reshape(n_blocks, MOE_BLOCK), block_expert))
    return out


def hybrid_layer(x, ctx, c, c_ctx, w_ada, b_ada, g_mix, g_ffn, w_in, attn_sink, w_gate2, b_gate2,
                 g_gla, w_out, w_router, b_router, w_gu, b_gu, w_down, b_down, update_ctx):
    b, n, d = x.shape
    mod = jax.nn.silu(c) @ w_ada + b_ada
    mod_c = jax.nn.silu(c_ctx) @ w_ada + b_ada
    sh_a, sc_a, gt_a, sh_f, sc_f, gt_f = jnp.split(mod[:, None, :], 6, axis=-1)
    csh_a, csc_a, cgt_a, csh_f, csc_f, cgt_f = jnp.split(mod_c, 6, axis=-1)

    h = modulate(rms_norm(x, g_mix), sh_a, sc_a)
    hc = modulate(rms_norm(ctx, g_mix), csh_a, csc_a)
    q_a, k_a, v_a, q_g, k_g, v_g, r_g, z_g = split_projection(h @ w_in)
    qc_a, kc_a, vc_a, qc_g, kc_g, vc_g, rc_g, zc_g = split_projection(hc @ w_in)
    cos, sin = axial_rope_angles(n)
    q_a = apply_axial_rope(q_a, cos, sin)
    k_a = apply_axial_rope(k_a, cos, sin)
    attn_lat = window_attention(q_a, k_a, v_a, kc_a, vc_a, attn_sink)
    o_lat, o_ctx = gla_bidirectional(q_g, k_g, v_g, z_g, qc_g, kc_g, vc_g, zc_g, w_gate2, b_gate2)
    gla_lat = gla_output(o_lat, r_g, g_gla)
    x = x + gt_a * (jnp.concatenate([attn_lat, gla_lat], axis=-1) @ w_out)
    if update_ctx:
        attn_ctx = context_attention(qc_a, kc_a, vc_a, attn_sink)
        gla_ctx = gla_output(o_ctx, rc_g, g_gla)
        ctx = ctx + cgt_a * (jnp.concatenate([attn_ctx, gla_ctx], axis=-1) @ w_out)

    h = modulate(rms_norm(x, g_ffn), sh_f, sc_f).reshape(b * n, d)
    if update_ctx:
        hc = modulate(rms_norm(ctx, g_ffn), csh_f, csc_f).reshape(-1, d)
        y = moe_ffn(jnp.concatenate([h, hc], axis=0), w_router, b_router, w_gu, b_gu, w_down, b_down)
        ctx = ctx + cgt_f * y[b * n:].reshape(ctx.shape)
        y = y[:b * n]
    else:
        y = moe_ffn(h, w_router, b_router, w_gu, b_gu, w_down, b_down)
    x = x + gt_f * y.reshape(b, n, d)
    return x, ctx


def setup_inputs(seed: int = 0) -> dict:
    key = jax.random.key(seed)
    ks = jax.random.split(key, 21)
    d = D_MODEL
    L = DEPTH
    gk = GLA_HEADS * GLA_DK

    def nrm(k, shape, scale):
        return jax.random.normal(k, shape, jnp.float32) * scale

    return {
        'x': nrm(ks[0], (BATCH, SEQ, d), 1.0),
        'c': nrm(ks[1], (BATCH, d), 1.0),
        'ctx': nrm(ks[2], (BATCH, CTX_LEN, d), 1.0),
        'c_ctx': nrm(ks[3], (d,), 1.0),
        'w_ada': nrm(ks[4], (L, d, 6 * d), 0.5 * d ** -0.5),
        'b_ada': nrm(ks[5], (L, 6 * d), 0.01),
        'g_mix': 1.0 + nrm(ks[6], (L, d), 0.1),
        'g_ffn': 1.0 + nrm(ks[7], (L, d), 0.1),
        'w_in': nrm(ks[8], (L, d, IN_WIDTH), d ** -0.5),
        'attn_sink': nrm(ks[9], (L, ATTN_HEADS), 0.5),
        'w_gate2': nrm(ks[10], (L, 2, GLA_GATE_RANK, gk), GLA_GATE_RANK ** -0.5),
        'b_gate2': nrm(ks[11], (L, 2, gk), 0.1),
        'g_gla': 1.0 + nrm(ks[12], (L, GLA_DV), 0.1),
        'w_out': nrm(ks[13], (L, MIX_WIDTH, d), MIX_WIDTH ** -0.5),
        'w_router': nrm(ks[14], (L, d, N_EXPERTS), d ** -0.5),
        'b_router': nrm(ks[15], (L, N_EXPERTS), 0.01),
        'w_gu': nrm(ks[16], (L, N_EXPERTS, d, 2 * D_EXPERT), d ** -0.5),
        'b_gu': nrm(ks[17], (L, N_EXPERTS, 2 * D_EXPERT), 0.01),
        'w_down': nrm(ks[18], (L, N_EXPERTS, D_EXPERT, d), D_EXPERT ** -0.5),
        'b_down': nrm(ks[19], (L, N_EXPERTS, d), 0.01),
        'g_final': 1.0 + nrm(ks[20], (d,), 0.1),
    }


def reference(x, c, ctx, c_ctx, w_ada, b_ada, g_mix, g_ffn, w_in, attn_sink, w_gate2, b_gate2, g_gla,
              w_out, w_router, b_router, w_gu, b_gu, w_down, b_down, g_final):
    for layer in range(DEPTH):
        x, ctx = hybrid_layer(x, ctx, c, c_ctx, w_ada[layer], b_ada[layer], g_mix[layer], g_ffn[layer],
                              w_in[layer], attn_sink[layer], w_gate2[layer], b_gate2[layer], g_gla[layer],
                              w_out[layer], w_router[layer], b_router[layer], w_gu[layer], b_gu[layer],
                              w_down[layer], b_down[layer], update_ctx=layer < DEPTH - 1)
    return rms_norm(x, g_final)
```

```python
import functools

import numpy as np
import jax
import jax.numpy as jnp
from jax import lax
from jax.experimental import pallas as pl
from jax.experimental.pallas import tpu as pltpu

F32 = jnp.float32
BF16 = jnp.bfloat16

GRID_W = 64
ATTN_HEADS = 16
ATTN_KV_HEADS = 4
HEAD_DIM = 64
GQA_GROUP = ATTN_HEADS // ATTN_KV_HEADS
ATTN_BLOCK = 128
ROPE_BASE = 10000.0
GLA_HEADS = 4
GLA_DK = 128
GLA_DV = 256
GLA_GATE_RANK = 16
GLA_GATE_TEMP = 16.0
GLA_CHUNK = 128
ATTN_WIDTH = ATTN_HEADS * HEAD_DIM
KV_WIDTH = ATTN_KV_HEADS * HEAD_DIM
GK_WIDTH = GLA_HEADS * GLA_DK
GLA_WIDTH = GLA_HEADS * GLA_DV
MAIN_WIDTH = ATTN_WIDTH + 2 * KV_WIDTH + 2 * GK_WIDTH + 2 * GLA_WIDTH
N_EXPERTS = 32
TOP_K = 4
SWIGLU_LIMIT = 7.0
SWIGLU_ALPHA = 1.702
NORM_EPS = 1e-6

LANES = 128
SUBLANES = 8
ROW_TILE = 256
ADA_COL_TILE = 1536
MOE_ROWS = 512
MOE_FF_TILE = 1024
VMEM_LIMIT = 56 * 1024 * 1024


def _split2(a):
    hi = a.astype(BF16)
    lo = (a - hi.astype(F32)).astype(BF16)
    return hi, lo


def _split3(a):
    hi = a.astype(BF16)
    r = a - hi.astype(F32)
    mid = r.astype(BF16)
    lo = (r - mid.astype(F32)).astype(BF16)
    return hi, mid, lo


def _dot(a, b):
    return jnp.dot(a, b, preferred_element_type=F32)


def _dot_nt(a, b):
    return lax.dot_general(a, b, (((1,), (1,)), ((), ())), preferred_element_type=F32)


def _dot_tn(a, b):
    return lax.dot_general(a, b, (((0,), (0,)), ((), ())), preferred_element_type=F32)


def _rms(x):
    return x * lax.rsqrt(jnp.mean(x * x, axis=-1, keepdims=True) + NORM_EPS)


def _ada_kernel(a_ref, w_ref, b_ref, o_ref):
    a = a_ref[...]
    a = a * jax.nn.sigmoid(a)
    o_ref[...] = _dot(a.astype(BF16), w_ref[...].astype(BF16)) + b_ref[...]


def _ada_call(cc, w_ada, b_ada):
    rows, d = cc.shape
    n = w_ada.shape[1]
    tn = ADA_COL_TILE
    return pl.pallas_call(
        _ada_kernel,
        grid=(n // tn,),
        in_specs=[pl.BlockSpec((rows, d), lambda j: (0, 0)),
                  pl.BlockSpec((d, tn), lambda j: (0, j)),
                  pl.BlockSpec((1, tn), lambda j: (0, j))],
        out_specs=pl.BlockSpec((rows, tn), lambda j: (0, j)),
        out_shape=jax.ShapeDtypeStruct((rows, n), F32),
        compiler_params=pltpu.CompilerParams(dimension_semantics=("arbitrary",), vmem_limit_bytes=VMEM_LIMIT),
        name="ada_mod",
    )(cc, w_ada, b_ada.reshape(1, n))


def _inproj_kernel(ctx_ref, x_ref, mod_ref, g_ref, cos_ref, sin_ref, wm_ref, wz_ref,
                   qa_ref, ka_ref, va_ref, qg_ref, kg_ref, vg_ref, rg_ref, zg_ref):
    t = pl.program_id(1)
    xin = jnp.where(t == 0, ctx_ref[0], x_ref[0])
    y = _rms(xin) * g_ref[...]
    h = (y * mod_ref[0, 0, 1:2, :] + mod_ref[0, 0, 0:1, :]).astype(BF16)
    p = _dot(h, wm_ref[...])
    cos = cos_ref[...]
    sin = sin_ref[...]
    lane = lax.broadcasted_iota(jnp.int32, cos.shape, 1)
    pair = HEAD_DIM // 4
    first_half = (lane & (2 * pair - 1)) < pair

    def rope(xs):
        partner = jnp.where(first_half, pltpu.roll(xs, LANES - pair, 1), pltpu.roll(xs, pair, 1))
        return xs * cos + partner * sin

    off = 0
    for i in range(ATTN_WIDTH // LANES):
        qa_ref[0, :, i * LANES:(i + 1) * LANES] = (
            rope(p[:, off + i * LANES:off + (i + 1) * LANES]) * HEAD_DIM ** -0.5).astype(BF16)
    off += ATTN_WIDTH
    for i in range(KV_WIDTH // LANES):
        ka_ref[0, :, i * LANES:(i + 1) * LANES] = rope(p[:, off + i * LANES:off + (i + 1) * LANES]).astype(BF16)
    off += KV_WIDTH
    va_ref[0] = p[:, off:off + KV_WIDTH].astype(BF16)
    off += KV_WIDTH
    qg_ref[0] = p[:, off:off + GK_WIDTH] * GLA_DK ** -0.5
    off += GK_WIDTH
    kg_ref[0] = p[:, off:off + GK_WIDTH]
    off += GK_WIDTH
    vg_ref[0] = p[:, off:off + GLA_WIDTH].astype(BF16)
    off += GLA_WIDTH
    rg_ref[0] = p[:, off:off + GLA_WIDTH]
    zg_ref[0] = _dot(h, wz_ref[...])


def _inproj_call(ctx, x, mod_a, g_mix, cos_t, sin_t, w_main, w_z):
    b, n, d = x.shape
    m = ctx.shape[1]
    tm = ROW_TILE
    assert m == tm and n % tm == 0
    nt = 1 + n // tm
    rows = m + n
    widths = [(ATTN_WIDTH, BF16), (KV_WIDTH, BF16), (KV_WIDTH, BF16), (GK_WIDTH, F32), (GK_WIDTH, F32),
              (GLA_WIDTH, BF16), (GLA_WIDTH, F32), (LANES, F32)]
    return pl.pallas_call(
        _inproj_kernel,
        grid=(b, nt),
        in_specs=[pl.BlockSpec((1, tm, d), lambda bi, t: (bi, 0, 0)),
                  pl.BlockSpec((1, tm, d), lambda bi, t: (bi, jnp.maximum(t - 1, 0), 0)),
                  pl.BlockSpec((1, 1, 2, d), lambda bi, t: (bi, jnp.minimum(t, 1), 0, 0)),
                  pl.BlockSpec((1, d), lambda bi, t: (0, 0)),
                  pl.BlockSpec((tm, LANES), lambda bi, t: (t, 0)),
                  pl.BlockSpec((tm, LANES), lambda bi, t: (t, 0)),
                  pl.BlockSpec((d, MAIN_WIDTH), lambda bi, t: (0, 0)),
                  pl.BlockSpec((d, LANES), lambda bi, t: (0, 0))],
        out_specs=[pl.BlockSpec((1, tm, w), lambda bi, t: (bi, t, 0)) for w, _ in widths],
        out_shape=[jax.ShapeDtypeStruct((b, rows, w), dt) for w, dt in widths],
        compiler_params=pltpu.CompilerParams(dimension_semantics=("arbitrary", "arbitrary"),
                                             vmem_limit_bytes=VMEM_LIMIT),
        name="in_proj",
    )(ctx, x, mod_a, g_mix.reshape(1, d), cos_t, sin_t, w_main, w_z)


def _attn_kernel(sink_ref, q_ref, kp_ref, kc_ref, kn_ref, kx_ref, vp_ref, vc_ref, vn_ref, vx_ref, o_ref):
    j = pl.program_id(1)
    nb = pl.num_programs(1)
    blk = ATTN_BLOCK
    q = q_ref[0]
    kall = jnp.concatenate([kp_ref[0], kc_ref[0], kn_ref[0], kx_ref[0]], axis=0)
    vall = jnp.concatenate([vp_ref[0], vc_ref[0], vn_ref[0], vx_ref[0]], axis=0)
    rows = GQA_GROUP * blk
    qi = lax.broadcasted_iota(jnp.int32, (rows, blk), 0) & (blk - 1)
    ki = lax.broadcasted_iota(jnp.int32, (rows, blk), 1)
    far = 4 * blk
    mask_prev = ki >= qi + jnp.where(j > 0, 0, far)
    mask_next = ki <= qi - jnp.where(j < nb - 1, 0, far)
    outs = []
    for kv in range(ATTN_KV_HEADS):
        heads = [kv * GQA_GROUP + g for g in range(GQA_GROUP)]
        qs = jnp.concatenate([q[:, h * HEAD_DIM:(h + 1) * HEAD_DIM] for h in heads], axis=0)
        kh = kall[:, kv * HEAD_DIM:(kv + 1) * HEAD_DIM]
        vh = vall[:, kv * HEAD_DIM:(kv + 1) * HEAD_DIM]
        s = _dot_nt(qs, kh)
        sp = jnp.where(mask_prev, s[:, :blk], -jnp.inf)
        sc = s[:, blk:2 * blk]
        sn = jnp.where(mask_next, s[:, 2 * blk:3 * blk], -jnp.inf)
        sx = s[:, 3 * blk:]
        sink = jnp.concatenate([jnp.full((blk, 1), sink_ref[h], F32) for h in heads], axis=0)
        m = jnp.maximum(jnp.maximum(jnp.max(sp, axis=-1, keepdims=True), jnp.max(sc, axis=-1, keepdims=True)),
                        jnp.maximum(jnp.max(sn, axis=-1, keepdims=True), jnp.max(sx, axis=-1, keepdims=True)))
        m = jnp.maximum(m, sink)
        e = jnp.concatenate([jnp.exp(sp - m), jnp.exp(sc - m), jnp.exp(sn - m), jnp.exp(sx - m)], axis=1)
        den = jnp.sum(e, axis=-1, keepdims=True) + jnp.exp(sink - m)
        o = _dot(e.astype(BF16), vh) / den
        outs += [o[g * blk:(g + 1) * blk] for g in range(GQA_GROUP)]
    o_ref[0] = jnp.concatenate(outs, axis=1).astype(BF16)


def _attn_call(sink, qa, ka, va, n):
    b = qa.shape[0]
    blk = ATTN_BLOCK
    nb = n // blk
    m = qa.shape[1] - n
    assert m % blk == 0
    mo = m // blk

    def kv_specs(width):
        return [pl.BlockSpec((1, blk, width), lambda bi, j: (bi, mo + jnp.maximum(j - 1, 0), 0)),
                pl.BlockSpec((1, blk, width), lambda bi, j: (bi, mo + j, 0)),
                pl.BlockSpec((1, blk, width), lambda bi, j: (bi, mo + jnp.minimum(j + 1, nb - 1), 0)),
                pl.BlockSpec((1, m, width), lambda bi, j: (bi, 0, 0))]

    return pl.pallas_call(
        _attn_kernel,
        grid=(b, nb),
        in_specs=[pl.BlockSpec(memory_space=pltpu.SMEM),
                  pl.BlockSpec((1, blk, ATTN_WIDTH), lambda bi, j: (bi, mo + j, 0))]
                 + kv_specs(KV_WIDTH) + kv_specs(KV_WIDTH),
        out_specs=pl.BlockSpec((1, blk, ATTN_WIDTH), lambda bi, j: (bi, j, 0)),
        out_shape=jax.ShapeDtypeStruct((b, n, ATTN_WIDTH), BF16),
        compiler_params=pltpu.CompilerParams(dimension_semantics=("arbitrary", "arbitrary"),
                                             vmem_limit_bytes=VMEM_LIMIT),
        name="window_attn",
    )(sink, qa, ka, ka, ka, ka, va, va, va, va)


def _gla_constants():
    c = GLA_CHUNK
    levels = int(np.log2(c))
    i = np.arange(c)[:, None]
    t = np.arange(c)[None, :]
    groups, qsides = [], []
    for l in range(levels):
        bs = c >> l
        hs = bs // 2
        mid = (i // bs) * bs + hs - 1
        later = (i % bs) >= hs
        groups.append(np.where(later, (t > mid) & (t <= i), (t > i) & (t <= mid)))
        qsides.append(np.broadcast_to(later, (c, GLA_DK)))
    groups.append(t <= i)
    groups.append(t > i)
    groups.append(np.ones((8, c), bool))
    fwd = np.concatenate(groups, 0)
    bwd = np.concatenate([g[::-1, ::-1] for g in groups], 0)
    qf = np.concatenate(qsides, 0)
    qb = np.concatenate([q[::-1] for q in qsides], 0)
    same = [(i // (c >> l)) == (t // (c >> l)) for l in range(levels)] + [i == t]
    return (np.stack([fwd, bwd]).astype(np.float32), np.stack([qf, qb]).astype(np.float32),
            np.stack(same).astype(np.float32))


def _gla_kernel(q_ref, k_ref, v_ref, z_ref, w2_ref, b2_ref, mall_ref, qsel_ref, same_ref, o_ref, st_ref,
                *, ctx_chunks):
    s = pl.program_id(3)
    c = GLA_CHUNK
    levels = same_ref.shape[0] - 1

    @pl.when(s == 0)
    def _():
        st_ref[...] = jnp.zeros_like(st_ref)

    z_hi, z_lo = _split2(z_ref[0])
    w_hi, w_lo = _split2(w2_ref[0])
    x = _dot(z_hi, w_hi) + _dot(z_hi, w_lo) + _dot(z_lo, w_hi) + b2_ref[0]
    g = (jnp.minimum(x, 0.0) - jnp.log1p(jnp.exp(-jnp.abs(x)))) * (1.0 / GLA_GATE_TEMP)
    mall = mall_ref[0]
    g_hi, g_mid, g_lo = _split3(g)
    e = jnp.exp(_dot(mall, g_hi) + _dot(mall, g_mid) + _dot(mall, g_lo))
    q = q_ref[0]
    k = k_ref[0]
    v = v_ref[0]
    wq = e[:levels * c] * qsel_ref[0]
    wk = e[:levels * c] - wq
    att = _dot_nt(q.astype(BF16), k.astype(BF16)) * same_ref[levels]
    for l in range(levels):
        ql = (q * wq[l * c:(l + 1) * c]).astype(BF16)
        kl = (k * wk[l * c:(l + 1) * c]).astype(BF16)
        att = att + _dot_nt(ql, kl) * same_ref[l]
    e_cum = e[levels * c:(levels + 1) * c]
    e_rest = e[(levels + 1) * c:(levels + 2) * c]
    e_tot = e[(levels + 2) * c:(levels + 2) * c + 1]
    st = st_ref[...]
    o = _dot_nt((q * e_cum).astype(BF16), st.astype(BF16)) + _dot(att.astype(BF16), v)
    st_ref[...] = st * e_tot + _dot_tn(v, (k * e_rest).astype(BF16))

    @pl.when(s >= ctx_chunks)
    def _():
        o_ref[0, 0] = o


def _gla_call(qg, kg, vg, zg, w2p, b2p, n):
    b = qg.shape[0]
    c = GLA_CHUNK
    total = qg.shape[1] // c
    nl = n // c
    nc = total - nl
    mall, qsel, same = _gla_constants()
    mall = jnp.asarray(mall, BF16)
    qsel = jnp.asarray(qsel, F32)
    same = jnp.asarray(same, F32)

    def chunk(d, s):
        back = jnp.where(s < nc, nc - 1 - s, total - 1 - (s - nc))
        return jnp.where(d == 0, s, back)

    def out_chunk(d, s):
        first = jnp.where(d == 0, 0, nl - 1)
        return jnp.where(s < nc, first, chunk(d, s) - nc)

    return pl.pallas_call(
        functools.partial(_gla_kernel, ctx_chunks=nc),
        grid=(2, b, GLA_HEADS, total),
        in_specs=[pl.BlockSpec((1, c, GLA_DK), lambda d, bi, h, s: (bi, chunk(d, s), h)),
                  pl.BlockSpec((1, c, GLA_DK), lambda d, bi, h, s: (bi, chunk(d, s), h)),
                  pl.BlockSpec((1, c, GLA_DV), lambda d, bi, h, s: (bi, chunk(d, s), h)),
                  pl.BlockSpec((1, c, LANES), lambda d, bi, h, s: (bi, chunk(d, s), 0)),
                  pl.BlockSpec((1, LANES, GLA_DK), lambda d, bi, h, s: (d, 0, h)),
                  pl.BlockSpec((1, 1, GLA_DK), lambda d, bi, h, s: (d, 0, h)),
                  pl.BlockSpec((1,) + mall.shape[1:], lambda d, bi, h, s: (d, 0, 0)),
                  pl.BlockSpec((1,) + qsel.shape[1:], lambda d, bi, h, s: (d, 0, 0)),
                  pl.BlockSpec(same.shape, lambda d, bi, h, s: (0, 0, 0))],
        out_specs=pl.BlockSpec((1, 1, c, GLA_DV), lambda d, bi, h, s: (d, bi, out_chunk(d, s), h)),
        out_shape=jax.ShapeDtypeStruct((2, b, n, GLA_WIDTH), F32),
        scratch_shapes=[pltpu.VMEM((GLA_DV, GLA_DK), F32)],
        compiler_params=pltpu.CompilerParams(
            dimension_semantics=("arbitrary", "arbitrary", "arbitrary", "arbitrary"), vmem_limit_bytes=VMEM_LIMIT),
        name="gla_scan",
    )(qg, kg, vg, zg, w2p, b2p, mall, qsel, same)


def _outproj_kernel(attn_ref, of_ref, ob_ref, r_ref, x_ref, wo_ref, ggla_ref, gta_ref, gffn_ref, modf_ref,
                    wr_ref, br_ref, x1_ref, h2_ref, idx_ref, gate_ref):
    o = of_ref[0, 0] + ob_ref[0, 0]
    r = r_ref[0]
    y = jnp.concatenate([_rms(o[:, h * GLA_DV:(h + 1) * GLA_DV]) for h in range(GLA_HEADS)], axis=1)
    gla = (y * ggla_ref[...] * (r * jax.nn.sigmoid(r))).astype(BF16)
    mix = _dot(attn_ref[0], wo_ref[:ATTN_WIDTH]) + _dot(gla, wo_ref[ATTN_WIDTH:])
    x1 = x_ref[0] + gta_ref[0] * mix
    x1_ref[0] = x1
    h2 = _rms(x1) * gffn_ref[...] * modf_ref[0, 1:2, :] + modf_ref[0, 0:1, :]
    h2_ref[0] = h2
    h_hi, h_lo = _split2(h2)
    w_hi, w_lo = _split2(wr_ref[...])
    vals = _dot_nt(w_hi, h_hi) + _dot_nt(w_hi, h_lo) + _dot_nt(w_lo, h_hi) + br_ref[...]
    ei = lax.broadcasted_iota(jnp.int32, vals.shape, 0)
    tops, ids = [], []
    for _ in range(TOP_K):
        m = jnp.max(vals, axis=0, keepdims=True)
        ix = jnp.min(jnp.where(vals == m, ei, N_EXPERTS), axis=0, keepdims=True)
        tops.append(m)
        ids.append(ix)
        vals = jnp.where(ei == ix, -jnp.inf, vals)
    es = [jnp.exp(m - tops[0]) for m in tops]
    den = es[0] + es[1] + es[2] + es[3]
    gate_ref[...] = jnp.concatenate([e / den for e in es], axis=0)
    idx_ref[...] = jnp.concatenate(ids, axis=0)


def _outproj_call(attn, o_gla, rg, x, w_out, g_gla_t, gt_a, g_ffn, mod_f, w_rt, b_r):
    b, n, d = x.shape
    tm = ROW_TILE
    nt = n // tm
    mo = (rg.shape[1] - n) // tm
    return pl.pallas_call(
        _outproj_kernel,
        grid=(b, nt),
        in_specs=[pl.BlockSpec((1, tm, ATTN_WIDTH), lambda bi, t: (bi, t, 0)),
                  pl.BlockSpec((1, 1, tm, GLA_WIDTH), lambda bi, t: (0, bi, t, 0)),
                  pl.BlockSpec((1, 1, tm, GLA_WIDTH), lambda bi, t: (1, bi, t, 0)),
                  pl.BlockSpec((1, tm, GLA_WIDTH), lambda bi, t: (bi, mo + t, 0)),
                  pl.BlockSpec((1, tm, d), lambda bi, t: (bi, t, 0)),
                  pl.BlockSpec(w_out.shape, lambda bi, t: (0, 0)),
                  pl.BlockSpec((1, GLA_WIDTH), lambda bi, t: (0, 0)),
                  pl.BlockSpec((1, 1, d), lambda bi, t: (bi, 0, 0)),
                  pl.BlockSpec((1, d), lambda bi, t: (0, 0)),
                  pl.BlockSpec((1, 2, d), lambda bi, t: (bi, 0, 0)),
                  pl.BlockSpec((N_EXPERTS, d), lambda bi, t: (0, 0)),
                  pl.BlockSpec((N_EXPERTS, 1), lambda bi, t: (0, 0))],
        out_specs=[pl.BlockSpec((1, tm, d), lambda bi, t: (bi, t, 0)),
                   pl.BlockSpec((1, tm, d), lambda bi, t: (bi, t, 0)),
                   pl.BlockSpec((TOP_K, tm), lambda bi, t: (0, bi * nt + t)),
                   pl.BlockSpec((TOP_K, tm), lambda bi, t: (0, bi * nt + t))],
        out_shape=[jax.ShapeDtypeStruct((b, n, d), F32), jax.ShapeDtypeStruct((b, n, d), F32),
                   jax.ShapeDtypeStruct((TOP_K, b * n), jnp.int32), jax.ShapeDtypeStruct((TOP_K, b * n), F32)],
        compiler_params=pltpu.CompilerParams(dimension_semantics=("arbitrary", "arbitrary"),
                                             vmem_limit_bytes=VMEM_LIMIT),
        name="out_proj_router",
    )(attn, o_gla, o_gla, rg, x, w_out, g_gla_t, gt_a, g_ffn.reshape(1, d), mod_f, w_rt, b_r)


def _route_metadata(top_idx, gates, rows_per_block):
    t = top_idx.shape[1]
    n_assign = t * TOP_K
    tm = rows_per_block
    expert = top_idx.T.reshape(n_assign)
    gate = gates.T.reshape(n_assign)
    order = jnp.argsort(expert).astype(jnp.int32)
    counts = jnp.bincount(expert, length=N_EXPERTS).astype(jnp.int32)
    start = jnp.cumsum(counts) - counts
    padded = (counts + tm - 1) // tm * tm
    pad_end = jnp.cumsum(padded)
    pad_start = pad_end - padded
    n_blocks = n_assign // tm + N_EXPERTS
    blk = jnp.arange(n_blocks, dtype=jnp.int32)
    block_expert = jnp.minimum(jnp.searchsorted(pad_end, blk * tm, side='right'), N_EXPERTS - 1).astype(jnp.int32)
    block_valid = jnp.clip(counts[block_expert] - (blk * tm - pad_start[block_expert]), 0, tm).astype(jnp.int32)
    row = jnp.arange(n_blocks * tm, dtype=jnp.int32)
    row_e = block_expert[row // tm]
    r = row - pad_start[row_e]
    valid = r < counts[row_e]
    a = order[jnp.clip(start[row_e] + r, 0, n_assign - 1)]
    src = jnp.where(valid, a // TOP_K, 0).astype(jnp.int32)
    dst = jnp.where(valid, a, 0).astype(jnp.int32)
    row_gate = jnp.where(valid, gate[a], 0.0).astype(F32)
    return (block_expert, block_valid, src.reshape(n_blocks, 1, tm), dst.reshape(n_blocks, 1, tm),
            row_gate.reshape(n_blocks, tm, 1))


def _moe_kernel(be_ref, bv_ref, src_ref, srcn_ref, dst_ref, gate_ref, h_hbm,
                wg_ref, wl_ref, bg_ref, bl_ref, wd_ref, bd_ref, y_hbm,
                xbuf, xb, acc, ybuf, gsem, ssem):
    i = pl.program_id(0)
    j = pl.program_id(1)
    nb = pl.num_programs(0)
    nf = pl.num_programs(1)
    tm = xb.shape[0]
    slot = i % 2
    nv = bv_ref[i]

    def start_gather(ids_ref, to_slot):
        def body(r, carry):
            pltpu.make_async_copy(h_hbm.at[pl.ds(ids_ref[0, 0, r], 1)], xbuf.at[to_slot, pl.ds(r, 1)],
                                  gsem.at[to_slot]).start()
            return carry
        lax.fori_loop(0, tm, body, 0)

    def wait_gather(in_slot):
        pltpu.make_async_copy(h_hbm.at[pl.ds(0, tm)], xbuf.at[in_slot], gsem.at[in_slot]).wait()

    def start_scatter(count):
        def body(r, carry):
            pltpu.make_async_copy(ybuf.at[pl.ds(r, 1)], y_hbm.at[pl.ds(dst_ref[0, 0, r], 1)], ssem.at[0]).start()
            return carry
        lax.fori_loop(0, count, body, 0)

    def wait_scatter(count):
        whole = pl.multiple_of(count // SUBLANES * SUBLANES, SUBLANES)

        @pl.when(whole > 0)
        def _():
            pltpu.make_async_copy(ybuf.at[pl.ds(0, whole)], y_hbm.at[pl.ds(0, whole)], ssem.at[0]).wait()

        def body(r, carry):
            pltpu.make_async_copy(ybuf.at[pl.ds(0, 1)], y_hbm.at[pl.ds(0, 1)], ssem.at[0]).wait()
            return carry
        lax.fori_loop(0, count % SUBLANES, body, 0)

    @pl.when(j == 0)
    def _():
        @pl.when(jnp.logical_and(i == 0, nv > 0))
        def _():
            start_gather(src_ref, 0)

        @pl.when(nv > 0)
        def _():
            wait_gather(slot)
            xb[...] = xbuf[slot].astype(BF16)

        nxt = jnp.minimum(i + 1, nb - 1)

        @pl.when(jnp.logical_and(i + 1 < nb, bv_ref[nxt] > 0))
        def _():
            start_gather(srcn_ref, 1 - slot)

    @pl.when(nv > 0)
    def _():
        x = xb[...]
        glu = jnp.minimum(_dot(x, wg_ref[0]) + bg_ref[0], SWIGLU_LIMIT)
        lin = jnp.clip(_dot(x, wl_ref[0]) + bl_ref[0], -SWIGLU_LIMIT, SWIGLU_LIMIT)
        act = glu * jax.nn.sigmoid(SWIGLU_ALPHA * glu) * (lin + 1.0)
        part = _dot(act.astype(BF16), wd_ref[0])

        @pl.when(j == 0)
        def _():
            acc[...] = part

        @pl.when(j > 0)
        def _():
            acc[...] += part

    @pl.when(j == nf - 1)
    def _():
        prev = jnp.maximum(i - 1, 0)

        @pl.when(jnp.logical_and(i > 0, bv_ref[prev] > 0))
        def _():
            wait_scatter(bv_ref[prev])

        @pl.when(nv > 0)
        def _():
            ybuf[...] = (acc[...] + bd_ref[0]) * gate_ref[0]
            start_scatter(nv)

        @pl.when(jnp.logical_and(i == nb - 1, nv > 0))
        def _():
            wait_scatter(nv)


def _moe_call(h2, meta, w_gu, b_gu, w_down, b_down):
    t, d = h2.shape
    block_expert, block_valid, src, dst, row_gate = meta
    nb, _, tm = src.shape
    ff = w_down.shape[1]
    tf = MOE_FF_TILE
    nf = ff // tf
    grid_spec = pltpu.PrefetchScalarGridSpec(
        num_scalar_prefetch=2,
        grid=(nb, nf),
        in_specs=[pl.BlockSpec((1, 1, tm), lambda i, j, be, bv: (i, 0, 0), memory_space=pltpu.SMEM),
                  pl.BlockSpec((1, 1, tm), lambda i, j, be, bv: (jnp.minimum(i + 1, nb - 1), 0, 0),
                               memory_space=pltpu.SMEM),
                  pl.BlockSpec((1, 1, tm), lambda i, j, be, bv: (i, 0, 0), memory_space=pltpu.SMEM),
                  pl.BlockSpec((1, tm, 1), lambda i, j, be, bv: (i, 0, 0)),
                  pl.BlockSpec(memory_space=pl.ANY),
                  pl.BlockSpec((1, d, tf), lambda i, j, be, bv: (be[i], 0, j)),
                  pl.BlockSpec((1, d, tf), lambda i, j, be, bv: (be[i], 0, nf + j)),
                  pl.BlockSpec((1, 1, tf), lambda i, j, be, bv: (be[i], 0, j)),
                  pl.BlockSpec((1, 1, tf), lambda i, j, be, bv: (be[i], 0, nf + j)),
                  pl.BlockSpec((1, tf, d), lambda i, j, be, bv: (be[i], j, 0)),
                  pl.BlockSpec((1, 1, d), lambda i, j, be, bv: (be[i], 0, 0))],
        out_specs=pl.BlockSpec(memory_space=pl.ANY),
        scratch_shapes=[pltpu.VMEM((2, tm, d), F32), pltpu.VMEM((tm, d), BF16), pltpu.VMEM((tm, d), F32),
                        pltpu.VMEM((tm, d), F32), pltpu.SemaphoreType.DMA((2,)), pltpu.SemaphoreType.DMA((1,))],
    )
    return pl.pallas_call(
        _moe_kernel,
        grid_spec=grid_spec,
        out_shape=jax.ShapeDtypeStruct((t * TOP_K, d), F32),
        compiler_params=pltpu.CompilerParams(dimension_semantics=("arbitrary", "arbitrary"),
                                             vmem_limit_bytes=VMEM_LIMIT, has_side_effects=True),
        name="moe_experts",
    )(block_expert, block_valid, src, src, dst, row_gate, h2,
      w_gu, w_gu, b_gu.reshape(N_EXPERTS, 1, 2 * ff), b_gu.reshape(N_EXPERTS, 1, 2 * ff),
      w_down, b_down.reshape(N_EXPERTS, 1, d))


def _combine_kernel(y_ref, x1_ref, gtf_ref, gfin_ref, o_ref):
    d = x1_ref.shape[-1]
    y = y_ref[:, 0:d]
    for kk in range(1, TOP_K):
        y = y + y_ref[:, kk * d:(kk + 1) * d]
    x2 = x1_ref[0] + gtf_ref[0] * y
    o_ref[0] = _rms(x2) * gfin_ref[...]


def _combine_call(y_rows, x1, gt_f, g_final):
    b, n, d = x1.shape
    tm = ROW_TILE
    nt = n // tm
    y = y_rows.reshape(y_rows.shape[0] // TOP_K, TOP_K * d)
    return pl.pallas_call(
        _combine_kernel,
        grid=(b, nt),
        in_specs=[pl.BlockSpec((tm, TOP_K * d), lambda bi, t: (bi * nt + t, 0)),
                  pl.BlockSpec((1, tm, d), lambda bi, t: (bi, t, 0)),
                  pl.BlockSpec((1, 1, d), lambda bi, t: (bi, 0, 0)),
                  pl.BlockSpec((1, d), lambda bi, t: (0, 0))],
        out_specs=pl.BlockSpec((1, tm, d), lambda bi, t: (bi, t, 0)),
        out_shape=jax.ShapeDtypeStruct((b, n, d), F32),
        compiler_params=pltpu.CompilerParams(dimension_semantics=("arbitrary", "arbitrary"),
                                             vmem_limit_bytes=VMEM_LIMIT),
        name="combine_norm",
    )(y, x1, gt_f, g_final.reshape(1, d))


def _rope_tables(n, m):
    pos = jnp.arange(n, dtype=jnp.int32)
    row = (pos // GRID_W).astype(F32)
    col = (pos % GRID_W).astype(F32)
    n_freq = HEAD_DIM // 4
    inv_freq = ROPE_BASE ** (-jnp.arange(n_freq, dtype=F32) / n_freq)
    ang_r = row[:, None] * inv_freq
    ang_c = col[:, None] * inv_freq
    cos = jnp.concatenate([jnp.cos(ang_r), jnp.cos(ang_r), jnp.cos(ang_c), jnp.cos(ang_c)], axis=1)
    sin = jnp.concatenate([-jnp.sin(ang_r), jnp.sin(ang_r), -jnp.sin(ang_c), jnp.sin(ang_c)], axis=1)
    cos = jnp.tile(cos, (1, LANES // HEAD_DIM))
    sin = jnp.tile(sin, (1, LANES // HEAD_DIM))
    cos = jnp.concatenate([jnp.ones((m, LANES), F32), cos], axis=0)
    sin = jnp.concatenate([jnp.zeros((m, LANES), F32), sin], axis=0)
    return cos, sin


def _layer(x, c, ctx, c_ctx, w_ada, b_ada, g_mix, g_ffn, w_in, attn_sink, w_gate2, b_gate2, g_gla,
           w_out, w_router, b_router, w_gu, b_gu, w_down, b_down, g_final):
    b, n, d = x.shape
    m = ctx.shape[1]

    cc = jnp.concatenate([c, c_ctx[None, :], jnp.zeros((8 - b - 1, d), F32)], axis=0)
    mod = _ada_call(cc, w_ada, b_ada)
    sh_a, sc_a, gt_a, sh_f, sc_f, gt_f = [mod[:, i * d:(i + 1) * d] for i in range(6)]
    lat = jnp.stack([sh_a[:b], 1.0 + sc_a[:b]], axis=1)
    cx = jnp.broadcast_to(jnp.stack([sh_a[b], 1.0 + sc_a[b]], axis=0)[None], (b, 2, d))
    mod_a = jnp.stack([cx, lat], axis=1)
    mod_f = jnp.stack([sh_f[:b], 1.0 + sc_f[:b]], axis=1)

    w_main = w_in[:, :MAIN_WIDTH].astype(BF16)
    w_z = jnp.pad(w_in[:, MAIN_WIDTH:], ((0, 0), (0, LANES - 2 * GLA_GATE_RANK))).astype(BF16)
    cos_t, sin_t = _rope_tables(n, m)
    qa, ka, va, qg, kg, vg, rg, zg = _inproj_call(ctx, x, mod_a, g_mix, cos_t, sin_t, w_main, w_z)

    attn = _attn_call(attn_sink, qa, ka, va, n)

    w2p = jnp.zeros((2, LANES, GK_WIDTH), F32)
    for dd in range(2):
        w2p = w2p.at[dd, dd * GLA_GATE_RANK:(dd + 1) * GLA_GATE_RANK].set(w_gate2[dd])
    o_gla = _gla_call(qg, kg, vg, zg, w2p, b_gate2.reshape(2, 1, GK_WIDTH), n)

    x1, h2, top_idx, gates = _outproj_call(
        attn, o_gla, rg, x, w_out.astype(BF16), jnp.tile(g_gla, GLA_HEADS).reshape(1, GLA_WIDTH),
        gt_a[:b, None, :], g_ffn, mod_f, w_router.T, b_router.reshape(N_EXPERTS, 1))

    meta = _route_metadata(top_idx, gates, MOE_ROWS)
    y_rows = _moe_call(h2.reshape(b * n, d), meta, w_gu.astype(BF16), b_gu, w_down.astype(BF16), b_down)
    return _combine_call(y_rows, x1, gt_f[:b, None, :], g_final)


def kernel(x, c, ctx, c_ctx, w_ada, b_ada, g_mix, g_ffn, w_in, attn_sink, w_gate2, b_gate2, g_gla, w_out,
           w_router, b_router, w_gu, b_gu, w_down, b_down, g_final):
    assert w_ada.shape[0] == 1, "single-layer problem: the context stream is never updated"
    return _layer(x, c, ctx, c_ctx, w_ada[0], b_ada[0], g_mix[0], g_ffn[0], w_in[0], attn_sink[0], w_gate2[0],
                  b_gate2[0], g_gla[0], w_out[0], w_router[0], b_router[0], w_gu[0], b_gu[0], w_down[0],
                  b_down[0], g_final)
```

```python
import functools

import numpy as np
import jax
import jax.numpy as jnp
from jax import lax
from jax.experimental import pallas as pl
from jax.experimental.pallas import tpu as pltpu

F32 = jnp.float32
BF16 = jnp.bfloat16

GRID_W = 64
ATTN_HEADS = 16
ATTN_KV_HEADS = 4
HEAD_DIM = 64
GQA_GROUP = ATTN_HEADS // ATTN_KV_HEADS
ATTN_BLOCK = 128
ROPE_BASE = 10000.0
GLA_HEADS = 4
GLA_DK = 128
GLA_DV = 256
GLA_GATE_RANK = 16
GLA_GATE_TEMP = 16.0
GLA_CHUNK = 128
ATTN_WIDTH = ATTN_HEADS * HEAD_DIM
KV_WIDTH = ATTN_KV_HEADS * HEAD_DIM
GK_WIDTH = GLA_HEADS * GLA_DK
GLA_WIDTH = GLA_HEADS * GLA_DV
MAIN_WIDTH = ATTN_WIDTH + 2 * KV_WIDTH + 2 * GK_WIDTH + 2 * GLA_WIDTH
N_EXPERTS = 32
TOP_K = 4
SWIGLU_LIMIT = 7.0
SWIGLU_ALPHA = 1.702
NORM_EPS = 1e-6

LANES = 128
SUBLANES = 8
ROW_TILE = 256
ADA_COL_TILE = 1536
MOE_ROWS = 512
VMEM_LIMIT = 56 * 1024 * 1024


def _split2(a):
    hi = a.astype(BF16)
    lo = (a - hi.astype(F32)).astype(BF16)
    return hi, lo


def _split3(a):
    hi = a.astype(BF16)
    r = a - hi.astype(F32)
    mid = r.astype(BF16)
    lo = (r - mid.astype(F32)).astype(BF16)
    return hi, mid, lo


def _dot(a, b):
    return jnp.dot(a, b, preferred_element_type=F32)


def _dot_nt(a, b):
    return lax.dot_general(a, b, (((1,), (1,)), ((), ())), preferred_element_type=F32)


def _dot_tn(a, b):
    return lax.dot_general(a, b, (((0,), (0,)), ((), ())), preferred_element_type=F32)


def _rms(x):
    return x * lax.rsqrt(jnp.mean(x * x, axis=-1, keepdims=True) + NORM_EPS)


def _ada_kernel(a_ref, w_ref, b_ref, o_ref):
    a = a_ref[...]
    a = a * jax.nn.sigmoid(a)
    o_ref[...] = _dot(a.astype(BF16), w_ref[...].astype(BF16)) + b_ref[...]


def _ada_call(cc, w_ada, b_ada):
    rows, d = cc.shape
    n = w_ada.shape[1]
    tn = ADA_COL_TILE
    return pl.pallas_call(
        _ada_kernel,
        grid=(n // tn,),
        in_specs=[pl.BlockSpec((rows, d), lambda j: (0, 0)),
                  pl.BlockSpec((d, tn), lambda j: (0, j)),
                  pl.BlockSpec((1, tn), lambda j: (0, j))],
        out_specs=pl.BlockSpec((rows, tn), lambda j: (0, j)),
        out_shape=jax.ShapeDtypeStruct((rows, n), F32),
        compiler_params=pltpu.CompilerParams(dimension_semantics=("arbitrary",), vmem_limit_bytes=VMEM_LIMIT),
        name="ada_mod",
    )(cc, w_ada, b_ada.reshape(1, n))


def _inproj_kernel(ctx_ref, x_ref, mod_ref, g_ref, cos_ref, sin_ref, wm_ref, wz_ref,
                   qa_ref, ka_ref, va_ref, qg_ref, kg_ref, vg_ref, rg_ref, zg_ref):
    t = pl.program_id(1)
    xin = jnp.where(t == 0, ctx_ref[0], x_ref[0])
    y = _rms(xin) * g_ref[...]
    h = (y * mod_ref[0, 0, 1:2, :] + mod_ref[0, 0, 0:1, :]).astype(BF16)
    p = _dot(h, wm_ref[...])
    cos = cos_ref[...]
    sin = sin_ref[...]
    lane = lax.broadcasted_iota(jnp.int32, cos.shape, 1)
    pair = HEAD_DIM // 4
    first_half = (lane & (2 * pair - 1)) < pair

    def rope(xs):
        partner = jnp.where(first_half, pltpu.roll(xs, LANES - pair, 1), pltpu.roll(xs, pair, 1))
        return xs * cos + partner * sin

    off = 0
    for i in range(ATTN_WIDTH // LANES):
        qa_ref[0, :, i * LANES:(i + 1) * LANES] = (
            rope(p[:, off + i * LANES:off + (i + 1) * LANES]) * HEAD_DIM ** -0.5).astype(BF16)
    off += ATTN_WIDTH
    for i in range(KV_WIDTH // LANES):
        ka_ref[0, :, i * LANES:(i + 1) * LANES] = rope(p[:, off + i * LANES:off + (i + 1) * LANES]).astype(BF16)
    off += KV_WIDTH
    va_ref[0] = p[:, off:off + KV_WIDTH].astype(BF16)
    off += KV_WIDTH
    qg_ref[0] = p[:, off:off + GK_WIDTH] * GLA_DK ** -0.5
    off += GK_WIDTH
    kg_ref[0] = p[:, off:off + GK_WIDTH]
    off += GK_WIDTH
    vg_ref[0] = p[:, off:off + GLA_WIDTH].astype(BF16)
    off += GLA_WIDTH
    rg_ref[0] = p[:, off:off + GLA_WIDTH]
    zg_ref[0] = _dot(h, wz_ref[...])


def _inproj_call(ctx, x, mod_a, g_mix, cos_t, sin_t, w_main, w_z):
    b, n, d = x.shape
    m = ctx.shape[1]
    tm = ROW_TILE
    assert m == tm and n % tm == 0
    nt = 1 + n // tm
    rows = m + n
    widths = [(ATTN_WIDTH, BF16), (KV_WIDTH, BF16), (KV_WIDTH, BF16), (GK_WIDTH, F32), (GK_WIDTH, F32),
              (GLA_WIDTH, BF16), (GLA_WIDTH, F32), (LANES, F32)]
    return pl.pallas_call(
        _inproj_kernel,
        grid=(b, nt),
        in_specs=[pl.BlockSpec((1, tm, d), lambda bi, t: (bi, 0, 0)),
                  pl.BlockSpec((1, tm, d), lambda bi, t: (bi, jnp.maximum(t - 1, 0), 0)),
                  pl.BlockSpec((1, 1, 2, d), lambda bi, t: (bi, jnp.minimum(t, 1), 0, 0)),
                  pl.BlockSpec((1, d), lambda bi, t: (0, 0)),
                  pl.BlockSpec((tm, LANES), lambda bi, t: (t, 0)),
                  pl.BlockSpec((tm, LANES), lambda bi, t: (t, 0)),
                  pl.BlockSpec((d, MAIN_WIDTH), lambda bi, t: (0, 0)),
                  pl.BlockSpec((d, LANES), lambda bi, t: (0, 0))],
        out_specs=[pl.BlockSpec((1, tm, w), lambda bi, t: (bi, t, 0)) for w, _ in widths],
        out_shape=[jax.ShapeDtypeStruct((b, rows, w), dt) for w, dt in widths],
        compiler_params=pltpu.CompilerParams(dimension_semantics=("arbitrary", "arbitrary"),
                                             vmem_limit_bytes=VMEM_LIMIT),
        name="in_proj",
    )(ctx, x, mod_a, g_mix.reshape(1, d), cos_t, sin_t, w_main, w_z)


def _attn_kernel(sink_ref, q_ref, kp_ref, kc_ref, kn_ref, kx_ref, vp_ref, vc_ref, vn_ref, vx_ref, o_ref):
    j = pl.program_id(1)
    nb = pl.num_programs(1)
    blk = ATTN_BLOCK
    q = q_ref[0]
    kall = jnp.concatenate([kp_ref[0], kc_ref[0], kn_ref[0], kx_ref[0]], axis=0)
    vall = jnp.concatenate([vp_ref[0], vc_ref[0], vn_ref[0], vx_ref[0]], axis=0)
    rows = GQA_GROUP * blk
    qi = lax.broadcasted_iota(jnp.int32, (rows, blk), 0) & (blk - 1)
    ki = lax.broadcasted_iota(jnp.int32, (rows, blk), 1)
    far = 4 * blk
    mask_prev = ki >= qi + jnp.where(j > 0, 0, far)
    mask_next = ki <= qi - jnp.where(j < nb - 1, 0, far)
    outs = []
    for kv in range(ATTN_KV_HEADS):
        heads = [kv * GQA_GROUP + g for g in range(GQA_GROUP)]
        qs = jnp.concatenate([q[:, h * HEAD_DIM:(h + 1) * HEAD_DIM] for h in heads], axis=0)
        kh = kall[:, kv * HEAD_DIM:(kv + 1) * HEAD_DIM]
        vh = vall[:, kv * HEAD_DIM:(kv + 1) * HEAD_DIM]
        s = _dot_nt(qs, kh)
        sp = jnp.where(mask_prev, s[:, :blk], -jnp.inf)
        sc = s[:, blk:2 * blk]
        sn = jnp.where(mask_next, s[:, 2 * blk:3 * blk], -jnp.inf)
        sx = s[:, 3 * blk:]
        sink = jnp.concatenate([jnp.full((blk, 1), sink_ref[h], F32) for h in heads], axis=0)
        m = jnp.maximum(jnp.maximum(jnp.max(sp, axis=-1, keepdims=True), jnp.max(sc, axis=-1, keepdims=True)),
                        jnp.maximum(jnp.max(sn, axis=-1, keepdims=True), jnp.max(sx, axis=-1, keepdims=True)))
        m = jnp.maximum(m, sink)
        e = jnp.concatenate([jnp.exp(sp - m), jnp.exp(sc - m), jnp.exp(sn - m), jnp.exp(sx - m)], axis=1)
        den = jnp.sum(e, axis=-1, keepdims=True) + jnp.exp(sink - m)
        o = _dot(e.astype(BF16), vh) / den
        outs += [o[g * blk:(g + 1) * blk] for g in range(GQA_GROUP)]
    o_ref[0] = jnp.concatenate(outs, axis=1).astype(BF16)


def _attn_call(sink, qa, ka, va, n):
    b = qa.shape[0]
    blk = ATTN_BLOCK
    nb = n // blk
    m = qa.shape[1] - n
    assert m % blk == 0
    mo = m // blk

    def kv_specs(width):
        return [pl.BlockSpec((1, blk, width), lambda bi, j: (bi, mo + jnp.maximum(j - 1, 0), 0)),
                pl.BlockSpec((1, blk, width), lambda bi, j: (bi, mo + j, 0)),
                pl.BlockSpec((1, blk, width), lambda bi, j: (bi, mo + jnp.minimum(j + 1, nb - 1), 0)),
                pl.BlockSpec((1, m, width), lambda bi, j: (bi, 0, 0))]

    return pl.pallas_call(
        _attn_kernel,
        grid=(b, nb),
        in_specs=[pl.BlockSpec(memory_space=pltpu.SMEM),
                  pl.BlockSpec((1, blk, ATTN_WIDTH), lambda bi, j: (bi, mo + j, 0))]
                 + kv_specs(KV_WIDTH) + kv_specs(KV_WIDTH),
        out_specs=pl.BlockSpec((1, blk, ATTN_WIDTH), lambda bi, j: (bi, j, 0)),
        out_shape=jax.ShapeDtypeStruct((b, n, ATTN_WIDTH), BF16),
        compiler_params=pltpu.CompilerParams(dimension_semantics=("arbitrary", "arbitrary"),
                                             vmem_limit_bytes=VMEM_LIMIT),
        name="window_attn",
    )(sink, qa, ka, ka, ka, ka, va, va, va, va)


def _gla_constants():
    c = GLA_CHUNK
    levels = int(np.log2(c))
    i = np.arange(c)[:, None]
    t = np.arange(c)[None, :]
    groups, qsides = [], []
    for l in range(levels):
        bs = c >> l
        hs = bs // 2
        mid = (i // bs) * bs + hs - 1
        later = (i % bs) >= hs
        groups.append(np.where(later, (t > mid) & (t <= i), (t > i) & (t <= mid)))
        qsides.append(np.broadcast_to(later, (c, GLA_DK)))
    groups.append(t <= i)
    groups.append(t > i)
    groups.append(np.ones((8, c), bool))
    fwd = np.concatenate(groups, 0)
    bwd = np.concatenate([g[::-1, ::-1] for g in groups], 0)
    qf = np.concatenate(qsides, 0)
    qb = np.concatenate([q[::-1] for q in qsides], 0)
    same = [(i // (c >> l)) == (t // (c >> l)) for l in range(levels)] + [i == t]
    return (np.stack([fwd, bwd]).astype(np.float32), np.stack([qf, qb]).astype(np.float32),
            np.stack(same).astype(np.float32))


def _gla_kernel(q_ref, k_ref, v_ref, z_ref, w2_ref, b2_ref, mall_ref, qsel_ref, same_ref, o_ref, st_ref,
                *, ctx_chunks):
    s = pl.program_id(2)
    c = GLA_CHUNK
    levels = same_ref.shape[0] - 1

    @pl.when(s == 0)
    def _():
        st_ref[...] = jnp.zeros_like(st_ref)

    z_hi, z_lo = _split2(z_ref[0])
    w_hi, w_lo = _split2(w2_ref[0])
    x = _dot(z_hi, w_hi) + _dot(z_hi, w_lo) + _dot(z_lo, w_hi) + b2_ref[0]
    g = (jnp.minimum(x, 0.0) - jnp.log1p(jnp.exp(-jnp.abs(x)))) * (1.0 / GLA_GATE_TEMP)
    mall = mall_ref[0]
    g_hi, g_lo = _split2(g)
    e_all = jnp.exp(_dot(mall, g_hi) + _dot(mall, g_lo))
    qsel = qsel_ref[0]
    outs = []
    for h in range(GLA_HEADS):
        e = e_all[:, h * GLA_DK:(h + 1) * GLA_DK]
        q = q_ref[0, :, h * GLA_DK:(h + 1) * GLA_DK]
        k = k_ref[0, :, h * GLA_DK:(h + 1) * GLA_DK]
        v = v_ref[0, :, h * GLA_DV:(h + 1) * GLA_DV]
        wq = e[:levels * c] * qsel
        wk = e[:levels * c] - wq
        att = _dot_nt(q.astype(BF16), k.astype(BF16)) * same_ref[levels]
        for l in range(levels):
            ql = (q * wq[l * c:(l + 1) * c]).astype(BF16)
            kl = (k * wk[l * c:(l + 1) * c]).astype(BF16)
            att = att + _dot_nt(ql, kl) * same_ref[l]
        e_cum = e[levels * c:(levels + 1) * c]
        e_rest = e[(levels + 1) * c:(levels + 2) * c]
        e_tot = e[(levels + 2) * c:(levels + 2) * c + 1]
        st = st_ref[h]
        outs.append(_dot_nt((q * e_cum).astype(BF16), st.astype(BF16)) + _dot(att.astype(BF16), v))
        st_ref[h] = st * e_tot + _dot_tn(v, (k * e_rest).astype(BF16))

    @pl.when(s >= ctx_chunks)
    def _():
        o_ref[0, 0] = jnp.concatenate(outs, axis=1)


def _gla_call(qg, kg, vg, zg, w2p, b2p, n):
    b = qg.shape[0]
    c = GLA_CHUNK
    total = qg.shape[1] // c
    nl = n // c
    nc = total - nl
    mall, qsel, same = _gla_constants()
    mall = jnp.asarray(mall, BF16)
    qsel = jnp.asarray(qsel, F32)
    same = jnp.asarray(same, F32)

    def chunk(d, s):
        back = jnp.where(s < nc, nc - 1 - s, total - 1 - (s - nc))
        return jnp.where(d == 0, s, back)

    def out_chunk(d, s):
        first = jnp.where(d == 0, 0, nl - 1)
        return jnp.where(s < nc, first, chunk(d, s) - nc)

    return pl.pallas_call(
        functools.partial(_gla_kernel, ctx_chunks=nc),
        grid=(2, b, total),
        in_specs=[pl.BlockSpec((1, c, GK_WIDTH), lambda d, bi, s: (bi, chunk(d, s), 0)),
                  pl.BlockSpec((1, c, GK_WIDTH), lambda d, bi, s: (bi, chunk(d, s), 0)),
                  pl.BlockSpec((1, c, GLA_WIDTH), lambda d, bi, s: (bi, chunk(d, s), 0)),
                  pl.BlockSpec((1, c, LANES), lambda d, bi, s: (bi, chunk(d, s), 0)),
                  pl.BlockSpec((1, LANES, GK_WIDTH), lambda d, bi, s: (d, 0, 0)),
                  pl.BlockSpec((1, 1, GK_WIDTH), lambda d, bi, s: (d, 0, 0)),
                  pl.BlockSpec((1,) + mall.shape[1:], lambda d, bi, s: (d, 0, 0)),
                  pl.BlockSpec((1,) + qsel.shape[1:], lambda d, bi, s: (d, 0, 0)),
                  pl.BlockSpec(same.shape, lambda d, bi, s: (0, 0, 0))],
        out_specs=pl.BlockSpec((1, 1, c, GLA_WIDTH), lambda d, bi, s: (d, bi, out_chunk(d, s), 0)),
        out_shape=jax.ShapeDtypeStruct((2, b, n, GLA_WIDTH), F32),
        scratch_shapes=[pltpu.VMEM((GLA_HEADS, GLA_DV, GLA_DK), F32)],
        compiler_params=pltpu.CompilerParams(
            dimension_semantics=("arbitrary", "arbitrary", "arbitrary"), vmem_limit_bytes=VMEM_LIMIT),
        name="gla_scan",
    )(qg, kg, vg, zg, w2p, b2p, mall, qsel, same)


def _outproj_kernel(attn_ref, of_ref, ob_ref, r_ref, x_ref, wo_ref, ggla_ref, gta_ref, gffn_ref, modf_ref,
                    wr_ref, br_ref, x1_ref, h2_ref, idx_ref, gate_ref):
    o = of_ref[0, 0] + ob_ref[0, 0]
    r = r_ref[0]
    y = jnp.concatenate([_rms(o[:, h * GLA_DV:(h + 1) * GLA_DV]) for h in range(GLA_HEADS)], axis=1)
    gla = (y * ggla_ref[...] * (r * jax.nn.sigmoid(r))).astype(BF16)
    mix = _dot(attn_ref[0], wo_ref[:ATTN_WIDTH]) + _dot(gla, wo_ref[ATTN_WIDTH:])
    x1 = x_ref[0] + gta_ref[0] * mix
    x1_ref[0] = x1
    h2 = _rms(x1) * gffn_ref[...] * modf_ref[0, 1:2, :] + modf_ref[0, 0:1, :]
    h2_ref[0] = h2
    h_hi, h_lo = _split2(h2)
    w_hi, w_lo = _split2(wr_ref[...])
    vals = _dot_nt(w_hi, h_hi) + _dot_nt(w_hi, h_lo) + _dot_nt(w_lo, h_hi) + br_ref[...]
    ei = lax.broadcasted_iota(jnp.int32, vals.shape, 0)
    tops, ids = [], []
    for _ in range(TOP_K):
        m = jnp.max(vals, axis=0, keepdims=True)
        ix = jnp.min(jnp.where(vals == m, ei, N_EXPERTS), axis=0, keepdims=True)
        tops.append(m)
        ids.append(ix)
        vals = jnp.where(ei == ix, -jnp.inf, vals)
    es = [jnp.exp(m - tops[0]) for m in tops]
    den = es[0] + es[1] + es[2] + es[3]
    gate_ref[...] = jnp.concatenate([e / den for e in es], axis=0)
    idx_ref[...] = jnp.concatenate(ids, axis=0)


def _outproj_call(attn, o_gla, rg, x, w_out, g_gla_t, gt_a, g_ffn, mod_f, w_rt, b_r):
    b, n, d = x.shape
    tm = ROW_TILE
    nt = n // tm
    mo = (rg.shape[1] - n) // tm
    return pl.pallas_call(
        _outproj_kernel,
        grid=(b, nt),
        in_specs=[pl.BlockSpec((1, tm, ATTN_WIDTH), lambda bi, t: (bi, t, 0)),
                  pl.BlockSpec((1, 1, tm, GLA_WIDTH), lambda bi, t: (0, bi, t, 0)),
                  pl.BlockSpec((1, 1, tm, GLA_WIDTH), lambda bi, t: (1, bi, t, 0)),
                  pl.BlockSpec((1, tm, GLA_WIDTH), lambda bi, t: (bi, mo + t, 0)),
                  pl.BlockSpec((1, tm, d), lambda bi, t: (bi, t, 0)),
                  pl.BlockSpec(w_out.shape, lambda bi, t: (0, 0)),
                  pl.BlockSpec((1, GLA_WIDTH), lambda bi, t: (0, 0)),
                  pl.BlockSpec((1, 1, d), lambda bi, t: (bi, 0, 0)),
                  pl.BlockSpec((1, d), lambda bi, t: (0, 0)),
                  pl.BlockSpec((1, 2, d), lambda bi, t: (bi, 0, 0)),
                  pl.BlockSpec((N_EXPERTS, d), lambda bi, t: (0, 0)),
                  pl.BlockSpec((N_EXPERTS, 1), lambda bi, t: (0, 0))],
        out_specs=[pl.BlockSpec((1, tm, d), lambda bi, t: (bi, t, 0)),
                   pl.BlockSpec((1, tm, d), lambda bi, t: (bi, t, 0)),
                   pl.BlockSpec((TOP_K, tm), lambda bi, t: (0, bi * nt + t)),
                   pl.BlockSpec((TOP_K, tm), lambda bi, t: (0, bi * nt + t))],
        out_shape=[jax.ShapeDtypeStruct((b, n, d), F32), jax.ShapeDtypeStruct((b, n, d), F32),
                   jax.ShapeDtypeStruct((TOP_K, b * n), jnp.int32), jax.ShapeDtypeStruct((TOP_K, b * n), F32)],
        compiler_params=pltpu.CompilerParams(dimension_semantics=("arbitrary", "arbitrary"),
                                             vmem_limit_bytes=VMEM_LIMIT),
        name="out_proj_router",
    )(attn, o_gla, o_gla, rg, x, w_out, g_gla_t, gt_a, g_ffn.reshape(1, d), mod_f, w_rt, b_r)


def _route_metadata(top_idx, gates, rows_per_block):
    t = top_idx.shape[1]
    n_assign = t * TOP_K
    tm = rows_per_block
    expert = top_idx.reshape(n_assign)
    gate = gates.reshape(n_assign)
    _, a_sorted, g_sorted = lax.sort((expert, jnp.arange(n_assign, dtype=jnp.int32), gate), num_keys=1,
                                     is_stable=True)
    counts = jnp.sum((expert[:, None] == jnp.arange(N_EXPERTS, dtype=jnp.int32)[None, :]).astype(jnp.int32), axis=0)
    start = jnp.cumsum(counts) - counts
    padded = (counts + tm - 1) // tm * tm
    pad_end = jnp.cumsum(padded)
    pad_start = pad_end - padded
    n_blocks = n_assign // tm + N_EXPERTS
    blk = jnp.arange(n_blocks, dtype=jnp.int32)
    block_expert = jnp.minimum(jnp.sum((pad_end[None, :] <= (blk * tm)[:, None]).astype(jnp.int32), axis=1),
                               N_EXPERTS - 1)
    offset = blk * tm - pad_start[block_expert]
    block_valid = jnp.clip(counts[block_expert] - offset, 0, tm).astype(jnp.int32)
    first = jnp.clip(start[block_expert] + offset, 0, n_assign)
    run = jax.vmap(lambda arr, s: lax.dynamic_slice(arr, (s,), (tm,)), in_axes=(None, 0))
    a_blk = run(jnp.concatenate([a_sorted, jnp.zeros((tm,), jnp.int32)]), first)
    g_blk = run(jnp.concatenate([g_sorted, jnp.zeros((tm,), F32)]), first)
    r = jnp.arange(tm, dtype=jnp.int32)[None, :]
    valid = r < block_valid[:, None]
    src = jnp.where(valid, a_blk % t, 0).astype(jnp.int32)
    spill = jnp.broadcast_to(n_assign + r, (1, tm)).astype(jnp.int32)
    dst = jnp.where(valid, a_blk, spill).astype(jnp.int32)
    dst_prev = jnp.concatenate([spill, dst[:-1]], axis=0)
    row_gate = jnp.where(valid, g_blk, 0.0).astype(F32)
    return (block_expert.astype(jnp.int32), block_valid, src.reshape(n_blocks, 1, tm),
            dst_prev.reshape(n_blocks, 1, tm), row_gate.reshape(n_blocks, tm, 1))


def _moe_kernel(be_ref, bv_ref, src_ref, srcn_ref, dstp_ref, gate_ref, h_hbm,
                wg_ref, wl_ref, bg_ref, bl_ref, wd_ref, bd_ref, y_hbm,
                xbuf, xb, acc, ybuf, gsem, ssem):
    i = pl.program_id(0)
    j = pl.program_id(1)
    tm = xb.shape[0]
    par = i % 2
    nv = bv_ref[i]
    prev_nv = bv_ref[jnp.maximum(i - 1, 0)]
    live = nv > 0
    first_empty = jnp.logical_and(jnp.logical_not(live), jnp.logical_and(i > 0, prev_nv > 0))

    def gather_row(ids_ref, r, to):
        return pltpu.make_async_copy(h_hbm.at[pl.ds(ids_ref[0, 0, r], 1)], xbuf.at[to, pl.ds(r, 1)], gsem.at[to])

    def scatter_row(r, frm):
        return pltpu.make_async_copy(ybuf.at[frm, pl.ds(r, 1)], y_hbm.at[pl.ds(dstp_ref[0, 0, r], 1)], ssem.at[frm])

    def wait_gather(slot):
        pltpu.make_async_copy(h_hbm.at[pl.ds(0, tm)], xbuf.at[slot], gsem.at[slot]).wait()

    def wait_scatter(slot):
        pltpu.make_async_copy(ybuf.at[slot], y_hbm.at[pl.ds(0, tm)], ssem.at[slot]).wait()

    def half_ffn():
        x = xb[...]
        glu = jnp.minimum(_dot(x, wg_ref[0]) + bg_ref[0], SWIGLU_LIMIT)
        lin = jnp.clip(_dot(x, wl_ref[0]) + bl_ref[0], -SWIGLU_LIMIT, SWIGLU_LIMIT)
        act = glu * jax.nn.sigmoid(SWIGLU_ALPHA * glu) * (lin + 1.0)
        return _dot(act.astype(BF16), wd_ref[0])

    @pl.when(jnp.logical_and(jnp.logical_and(i == 0, j == 0), live))
    def _():
        ybuf[1] = jnp.zeros(ybuf.shape[1:], ybuf.dtype)

        def body(r, carry):
            gather_row(src_ref, r, 0).start()
            return carry
        lax.fori_loop(0, tm, body, 0)

    @pl.when(jnp.logical_and(j == 0, jnp.logical_or(live, first_empty)))
    def _():
        wait_gather(par)

    @pl.when(jnp.logical_and(j == 0, live))
    def _():
        xb[...] = xbuf[par].astype(BF16)
        acc[...] = half_ffn()
        for r in range(tm):
            gather_row(srcn_ref, r, 1 - par).start()

    @pl.when(jnp.logical_and(j == 1, live))
    def _():
        part = half_ffn()
        for r in range(tm):
            scatter_row(r, 1 - par).start()

        @pl.when(i >= 1)
        def _():
            wait_scatter(par)

        ybuf[par] = (acc[...] + part + bd_ref[0]) * gate_ref[0]

    @pl.when(jnp.logical_and(j == 1, first_empty))
    def _():
        def body(r, carry):
            scatter_row(r, 1 - par).start()
            return carry
        lax.fori_loop(0, tm, body, 0)
        wait_scatter(par)
        wait_scatter(1 - par)


def _moe_call(h2, meta, w_gu, b_gu, w_down, b_down):
    t, d = h2.shape
    block_expert, block_valid, src, dst_prev, row_gate = meta
    nb, _, tm = src.shape
    ff = w_down.shape[1]
    nf = 2
    tf = ff // nf
    grid_spec = pltpu.PrefetchScalarGridSpec(
        num_scalar_prefetch=2,
        grid=(nb, nf),
        in_specs=[pl.BlockSpec((1, 1, tm), lambda i, j, be, bv: (i, 0, 0), memory_space=pltpu.SMEM),
                  pl.BlockSpec((1, 1, tm), lambda i, j, be, bv: (jnp.minimum(i + 1, nb - 1), 0, 0),
                               memory_space=pltpu.SMEM),
                  pl.BlockSpec((1, 1, tm), lambda i, j, be, bv: (i, 0, 0), memory_space=pltpu.SMEM),
                  pl.BlockSpec((1, tm, 1), lambda i, j, be, bv: (i, 0, 0)),
                  pl.BlockSpec(memory_space=pl.ANY),
                  pl.BlockSpec((1, d, tf), lambda i, j, be, bv: (be[i], 0, j)),
                  pl.BlockSpec((1, d, tf), lambda i, j, be, bv: (be[i], 0, nf + j)),
                  pl.BlockSpec((1, 1, tf), lambda i, j, be, bv: (be[i], 0, j)),
                  pl.BlockSpec((1, 1, tf), lambda i, j, be, bv: (be[i], 0, nf + j)),
                  pl.BlockSpec((1, tf, d), lambda i, j, be, bv: (be[i], j, 0)),
                  pl.BlockSpec((1, 1, d), lambda i, j, be, bv: (be[i], 0, 0))],
        out_specs=pl.BlockSpec(memory_space=pl.ANY),
        scratch_shapes=[pltpu.VMEM((2, tm, d), F32), pltpu.VMEM((tm, d), BF16), pltpu.VMEM((tm, d), F32),
                        pltpu.VMEM((2, tm, d), F32), pltpu.SemaphoreType.DMA((2,)), pltpu.SemaphoreType.DMA((2,))],
    )
    return pl.pallas_call(
        _moe_kernel,
        grid_spec=grid_spec,
        out_shape=jax.ShapeDtypeStruct((t * TOP_K + tm, d), F32),
        compiler_params=pltpu.CompilerParams(dimension_semantics=("arbitrary", "arbitrary"),
                                             vmem_limit_bytes=VMEM_LIMIT, has_side_effects=True),
        name="moe_experts",
    )(block_expert, block_valid, src, src, dst_prev, row_gate, h2,
      w_gu, w_gu, b_gu.reshape(N_EXPERTS, 1, 2 * ff), b_gu.reshape(N_EXPERTS, 1, 2 * ff),
      w_down, b_down.reshape(N_EXPERTS, 1, d))


def _combine_kernel(y0_ref, y1_ref, y2_ref, y3_ref, x1_ref, gtf_ref, gfin_ref, o_ref):
    y = (y0_ref[...] + y1_ref[...]) + (y2_ref[...] + y3_ref[...])
    x2 = x1_ref[0] + gtf_ref[0] * y
    o_ref[0] = _rms(x2) * gfin_ref[...]


def _combine_call(y_rows, x1, gt_f, g_final):
    b, n, d = x1.shape
    tm = ROW_TILE
    nt = n // tm
    per_slot = b * nt
    assert TOP_K == 4

    def slot_spec(kk):
        return pl.BlockSpec((tm, d), lambda bi, t: (kk * per_slot + bi * nt + t, 0))

    return pl.pallas_call(
        _combine_kernel,
        grid=(b, nt),
        in_specs=[slot_spec(0), slot_spec(1), slot_spec(2), slot_spec(3),
                  pl.BlockSpec((1, tm, d), lambda bi, t: (bi, t, 0)),
                  pl.BlockSpec((1, 1, d), lambda bi, t: (bi, 0, 0)),
                  pl.BlockSpec((1, d), lambda bi, t: (0, 0))],
        out_specs=pl.BlockSpec((1, tm, d), lambda bi, t: (bi, t, 0)),
        out_shape=jax.ShapeDtypeStruct((b, n, d), F32),
        compiler_params=pltpu.CompilerParams(dimension_semantics=("arbitrary", "arbitrary"),
                                             vmem_limit_bytes=VMEM_LIMIT),
        name="combine_norm",
    )(y_rows, y_rows, y_rows, y_rows, x1, gt_f, g_final.reshape(1, d))


def _rope_tables(n, m):
    pos = jnp.arange(n, dtype=jnp.int32)
    row = (pos // GRID_W).astype(F32)
    col = (pos % GRID_W).astype(F32)
    n_freq = HEAD_DIM // 4
    inv_freq = ROPE_BASE ** (-jnp.arange(n_freq, dtype=F32) / n_freq)
    ang_r = row[:, None] * inv_freq
    ang_c = col[:, None] * inv_freq
    cos = jnp.concatenate([jnp.cos(ang_r), jnp.cos(ang_r), jnp.cos(ang_c), jnp.cos(ang_c)], axis=1)
    sin = jnp.concatenate([-jnp.sin(ang_r), jnp.sin(ang_r), -jnp.sin(ang_c), jnp.sin(ang_c)], axis=1)
    cos = jnp.tile(cos, (1, LANES // HEAD_DIM))
    sin = jnp.tile(sin, (1, LANES // HEAD_DIM))
    cos = jnp.concatenate([jnp.ones((m, LANES), F32), cos], axis=0)
    sin = jnp.concatenate([jnp.zeros((m, LANES), F32), sin], axis=0)
    return cos, sin


def _layer(x, c, ctx, c_ctx, w_ada, b_ada, g_mix, g_ffn, w_in, attn_sink, w_gate2, b_gate2, g_gla,
           w_out, w_router, b_router, w_gu, b_gu, w_down, b_down, g_final):
    b, n, d = x.shape
    m = ctx.shape[1]

    cc = jnp.concatenate([c, c_ctx[None, :], jnp.zeros((8 - b - 1, d), F32)], axis=0)
    mod = _ada_call(cc, w_ada, b_ada)
    sh_a, sc_a, gt_a, sh_f, sc_f, gt_f = [mod[:, i * d:(i + 1) * d] for i in range(6)]
    lat = jnp.stack([sh_a[:b], 1.0 + sc_a[:b]], axis=1)
    cx = jnp.broadcast_to(jnp.stack([sh_a[b], 1.0 + sc_a[b]], axis=0)[None], (b, 2, d))
    mod_a = jnp.stack([cx, lat], axis=1)
    mod_f = jnp.stack([sh_f[:b], 1.0 + sc_f[:b]], axis=1)

    w_main = w_in[:, :MAIN_WIDTH].astype(BF16)
    w_z = jnp.pad(w_in[:, MAIN_WIDTH:], ((0, 0), (0, LANES - 2 * GLA_GATE_RANK))).astype(BF16)
    cos_t, sin_t = _rope_tables(n, m)
    qa, ka, va, qg, kg, vg, rg, zg = _inproj_call(ctx, x, mod_a, g_mix, cos_t, sin_t, w_main, w_z)

    attn = _attn_call(attn_sink, qa, ka, va, n)

    w2p = jnp.zeros((2, LANES, GK_WIDTH), F32)
    for dd in range(2):
        w2p = w2p.at[dd, dd * GLA_GATE_RANK:(dd + 1) * GLA_GATE_RANK].set(w_gate2[dd])
    o_gla = _gla_call(qg, kg, vg, zg, w2p, b_gate2.reshape(2, 1, GK_WIDTH), n)

    x1, h2, top_idx, gates = _outproj_call(
        attn, o_gla, rg, x, w_out.astype(BF16), jnp.tile(g_gla, GLA_HEADS).reshape(1, GLA_WIDTH),
        gt_a[:b, None, :], g_ffn, mod_f, w_router.T, b_router.reshape(N_EXPERTS, 1))

    meta = _route_metadata(top_idx, gates, MOE_ROWS)
    y_rows = _moe_call(h2.reshape(b * n, d), meta, w_gu.astype(BF16), b_gu, w_down.astype(BF16), b_down)
    return _combine_call(y_rows, x1, gt_f[:b, None, :], g_final)


def kernel(x, c, ctx, c_ctx, w_ada, b_ada, g_mix, g_ffn, w_in, attn_sink, w_gate2, b_gate2, g_gla, w_out,
           w_router, b_router, w_gu, b_gu, w_down, b_down, g_final):
    assert w_ada.shape[0] == 1, "single-layer problem: the context stream is never updated"
    return _layer(x, c, ctx, c_ctx, w_ada[0], b_ada[0], g_mix[0], g_ffn[0], w_in[0], attn_sink[0], w_gate2[0],
                  b_gate2[0], g_gla[0], w_out[0], w_router[0], b_router[0], w_gu[0], b_gu[0], w_down[0],
                  b_down[0], g_final)
```

```python
import functools

import numpy as np
import jax
import jax.numpy as jnp
from jax import lax
from jax.experimental import pallas as pl
from jax.experimental.pallas import tpu as pltpu

F32 = jnp.float32
BF16 = jnp.bfloat16

GRID_W = 64
ATTN_HEADS = 16
ATTN_KV_HEADS = 4
HEAD_DIM = 64
GQA_GROUP = ATTN_HEADS // ATTN_KV_HEADS
ATTN_BLOCK = 128
ROPE_BASE = 10000.0
GLA_HEADS = 4
GLA_DK = 128
GLA_DV = 256
GLA_GATE_RANK = 16
GLA_GATE_TEMP = 16.0
GLA_CHUNK = 128
ATTN_WIDTH = ATTN_HEADS * HEAD_DIM
KV_WIDTH = ATTN_KV_HEADS * HEAD_DIM
GK_WIDTH = GLA_HEADS * GLA_DK
GLA_WIDTH = GLA_HEADS * GLA_DV
MAIN_WIDTH = ATTN_WIDTH + 2 * KV_WIDTH + 2 * GK_WIDTH + 2 * GLA_WIDTH
N_EXPERTS = 32
TOP_K = 4
SWIGLU_LIMIT = 7.0
SWIGLU_ALPHA = 1.702
NORM_EPS = 1e-6

LANES = 128
SUBLANES = 8
ROW_TILE = 256
ADA_COL_TILE = 1536
MOE_ROWS = 512
VMEM_LIMIT = 56 * 1024 * 1024


def _split2(a):
    hi = a.astype(BF16)
    lo = (a - hi.astype(F32)).astype(BF16)
    return hi, lo


def _split3(a):
    hi = a.astype(BF16)
    r = a - hi.astype(F32)
    mid = r.astype(BF16)
    lo = (r - mid.astype(F32)).astype(BF16)
    return hi, mid, lo


def _dot(a, b):
    return jnp.dot(a, b, preferred_element_type=F32)


def _dot_nt(a, b):
    return lax.dot_general(a, b, (((1,), (1,)), ((), ())), preferred_element_type=F32)


def _dot_tn(a, b):
    return lax.dot_general(a, b, (((0,), (0,)), ((), ())), preferred_element_type=F32)


def _rms(x):
    return x * lax.rsqrt(jnp.mean(x * x, axis=-1, keepdims=True) + NORM_EPS)


def _ada_kernel(a_ref, w_ref, b_ref, o_ref):
    a = a_ref[...]
    a = a * jax.nn.sigmoid(a)
    o_ref[...] = _dot(a.astype(BF16), w_ref[...].astype(BF16)) + b_ref[...]


def _ada_call(cc, w_ada, b_ada):
    rows, d = cc.shape
    n = w_ada.shape[1]
    tn = ADA_COL_TILE
    return pl.pallas_call(
        _ada_kernel,
        grid=(n // tn,),
        in_specs=[pl.BlockSpec((rows, d), lambda j: (0, 0)),
                  pl.BlockSpec((d, tn), lambda j: (0, j)),
                  pl.BlockSpec((1, tn), lambda j: (0, j))],
        out_specs=pl.BlockSpec((rows, tn), lambda j: (0, j)),
        out_shape=jax.ShapeDtypeStruct((rows, n), F32),
        compiler_params=pltpu.CompilerParams(dimension_semantics=("arbitrary",), vmem_limit_bytes=VMEM_LIMIT),
        name="ada_mod",
    )(cc, w_ada, b_ada.reshape(1, n))


def _inproj_kernel(ctx_ref, x_ref, mod_ref, g_ref, cos_ref, sin_ref, wm_ref, wz_ref,
                   qa_ref, ka_ref, va_ref, qg_ref, kg_ref, vg_ref, rg_ref, zg_ref):
    t = pl.program_id(1)
    xin = jnp.where(t == 0, ctx_ref[0], x_ref[0])
    y = _rms(xin) * g_ref[...]
    h = (y * mod_ref[0, 0, 1:2, :] + mod_ref[0, 0, 0:1, :]).astype(BF16)
    p = _dot(h, wm_ref[...])
    cos = cos_ref[...]
    sin = sin_ref[...]
    lane = lax.broadcasted_iota(jnp.int32, cos.shape, 1)
    pair = HEAD_DIM // 4
    first_half = (lane & (2 * pair - 1)) < pair

    def rope(xs):
        partner = jnp.where(first_half, pltpu.roll(xs, LANES - pair, 1), pltpu.roll(xs, pair, 1))
        return xs * cos + partner * sin

    off = 0
    for i in range(ATTN_WIDTH // LANES):
        qa_ref[0, :, i * LANES:(i + 1) * LANES] = (
            rope(p[:, off + i * LANES:off + (i + 1) * LANES]) * HEAD_DIM ** -0.5).astype(BF16)
    off += ATTN_WIDTH
    for i in range(KV_WIDTH // LANES):
        ka_ref[0, :, i * LANES:(i + 1) * LANES] = rope(p[:, off + i * LANES:off + (i + 1) * LANES]).astype(BF16)
    off += KV_WIDTH
    va_ref[0] = p[:, off:off + KV_WIDTH].astype(BF16)
    off += KV_WIDTH
    qg_ref[0] = p[:, off:off + GK_WIDTH] * GLA_DK ** -0.5
    off += GK_WIDTH
    kg_ref[0] = p[:, off:off + GK_WIDTH]
    off += GK_WIDTH
    vg_ref[0] = p[:, off:off + GLA_WIDTH].astype(BF16)
    off += GLA_WIDTH
    rg_ref[0] = p[:, off:off + GLA_WIDTH]
    zg_ref[0] = _dot(h, wz_ref[...])


def _inproj_call(ctx, x, mod_a, g_mix, cos_t, sin_t, w_main, w_z):
    b, n, d = x.shape
    m = ctx.shape[1]
    tm = ROW_TILE
    assert m == tm and n % tm == 0
    nt = 1 + n // tm
    rows = m + n
    widths = [(ATTN_WIDTH, BF16), (KV_WIDTH, BF16), (KV_WIDTH, BF16), (GK_WIDTH, F32), (GK_WIDTH, F32),
              (GLA_WIDTH, BF16), (GLA_WIDTH, F32), (LANES, F32)]
    return pl.pallas_call(
        _inproj_kernel,
        grid=(b, nt),
        in_specs=[pl.BlockSpec((1, tm, d), lambda bi, t: (bi, 0, 0)),
                  pl.BlockSpec((1, tm, d), lambda bi, t: (bi, jnp.maximum(t - 1, 0), 0)),
                  pl.BlockSpec((1, 1, 2, d), lambda bi, t: (bi, jnp.minimum(t, 1), 0, 0)),
                  pl.BlockSpec((1, d), lambda bi, t: (0, 0)),
                  pl.BlockSpec((tm, LANES), lambda bi, t: (t, 0)),
                  pl.BlockSpec((tm, LANES), lambda bi, t: (t, 0)),
                  pl.BlockSpec((d, MAIN_WIDTH), lambda bi, t: (0, 0)),
                  pl.BlockSpec((d, LANES), lambda bi, t: (0, 0))],
        out_specs=[pl.BlockSpec((1, tm, w), lambda bi, t: (bi, t, 0)) for w, _ in widths],
        out_shape=[jax.ShapeDtypeStruct((b, rows, w), dt) for w, dt in widths],
        compiler_params=pltpu.CompilerParams(dimension_semantics=("arbitrary", "arbitrary"),
                                             vmem_limit_bytes=VMEM_LIMIT),
        name="in_proj",
    )(ctx, x, mod_a, g_mix.reshape(1, d), cos_t, sin_t, w_main, w_z)


def _attn_kernel(sink_ref, q_ref, kp_ref, kc_ref, kn_ref, kx_ref, vp_ref, vc_ref, vn_ref, vx_ref, o_ref):
    j = pl.program_id(1)
    nb = pl.num_programs(1)
    blk = ATTN_BLOCK
    q = q_ref[0]
    kall = jnp.concatenate([kp_ref[0], kc_ref[0], kn_ref[0], kx_ref[0]], axis=0)
    vall = jnp.concatenate([vp_ref[0], vc_ref[0], vn_ref[0], vx_ref[0]], axis=0)
    rows = GQA_GROUP * blk
    qi = lax.broadcasted_iota(jnp.int32, (rows, blk), 0) & (blk - 1)
    ki = lax.broadcasted_iota(jnp.int32, (rows, blk), 1)
    far = 4 * blk
    mask_prev = ki >= qi + jnp.where(j > 0, 0, far)
    mask_next = ki <= qi - jnp.where(j < nb - 1, 0, far)
    outs = []
    for kv in range(ATTN_KV_HEADS):
        heads = [kv * GQA_GROUP + g for g in range(GQA_GROUP)]
        qs = jnp.concatenate([q[:, h * HEAD_DIM:(h + 1) * HEAD_DIM] for h in heads], axis=0)
        kh = kall[:, kv * HEAD_DIM:(kv + 1) * HEAD_DIM]
        vh = vall[:, kv * HEAD_DIM:(kv + 1) * HEAD_DIM]
        s = _dot_nt(qs, kh)
        sp = jnp.where(mask_prev, s[:, :blk], -jnp.inf)
        sc = s[:, blk:2 * blk]
        sn = jnp.where(mask_next, s[:, 2 * blk:3 * blk], -jnp.inf)
        sx = s[:, 3 * blk:]
        sink = jnp.concatenate([jnp.full((blk, 1), sink_ref[h], F32) for h in heads], axis=0)
        folded = jnp.maximum(jnp.maximum(sp, sc), sn)
        for t0 in range(0, sx.shape[1], blk):
            folded = jnp.maximum(folded, sx[:, t0:t0 + blk])
        m = jnp.maximum(jnp.max(folded, axis=-1, keepdims=True), sink)
        e = jnp.concatenate([jnp.exp(sp - m), jnp.exp(sc - m), jnp.exp(sn - m), jnp.exp(sx - m)], axis=1)
        den = jnp.sum(e, axis=-1, keepdims=True) + jnp.exp(sink - m)
        o = _dot(e.astype(BF16), vh) / den
        outs += [o[g * blk:(g + 1) * blk] for g in range(GQA_GROUP)]
    o_ref[0] = jnp.concatenate(outs, axis=1).astype(BF16)


def _attn_call(sink, qa, ka, va, n):
    b = qa.shape[0]
    blk = ATTN_BLOCK
    nb = n // blk
    m = qa.shape[1] - n
    assert m % blk == 0
    mo = m // blk

    def kv_specs(width):
        return [pl.BlockSpec((1, blk, width), lambda bi, j: (bi, mo + jnp.maximum(j - 1, 0), 0)),
                pl.BlockSpec((1, blk, width), lambda bi, j: (bi, mo + j, 0)),
                pl.BlockSpec((1, blk, width), lambda bi, j: (bi, mo + jnp.minimum(j + 1, nb - 1), 0)),
                pl.BlockSpec((1, m, width), lambda bi, j: (bi, 0, 0))]

    return pl.pallas_call(
        _attn_kernel,
        grid=(b, nb),
        in_specs=[pl.BlockSpec(memory_space=pltpu.SMEM),
                  pl.BlockSpec((1, blk, ATTN_WIDTH), lambda bi, j: (bi, mo + j, 0))]
                 + kv_specs(KV_WIDTH) + kv_specs(KV_WIDTH),
        out_specs=pl.BlockSpec((1, blk, ATTN_WIDTH), lambda bi, j: (bi, j, 0)),
        out_shape=jax.ShapeDtypeStruct((b, n, ATTN_WIDTH), BF16),
        compiler_params=pltpu.CompilerParams(dimension_semantics=("arbitrary", "arbitrary"),
                                             vmem_limit_bytes=VMEM_LIMIT),
        name="window_attn",
    )(sink, qa, ka, ka, ka, ka, va, va, va, va)


def _gla_constants():
    c = GLA_CHUNK
    levels = int(np.log2(c))
    i = np.arange(c)[:, None]
    t = np.arange(c)[None, :]
    groups, qsides = [], []
    for l in range(levels):
        bs = c >> l
        hs = bs // 2
        mid = (i // bs) * bs + hs - 1
        later = (i % bs) >= hs
        groups.append(np.where(later, (t > mid) & (t <= i), (t > i) & (t <= mid)))
        qsides.append(np.broadcast_to(later, (c, GLA_DK)))
    groups.append(t <= i)
    groups.append(t > i)
    groups.append(np.ones((8, c), bool))
    fwd = np.concatenate(groups, 0)
    bwd = np.concatenate([g[::-1, ::-1] for g in groups], 0)
    qf = np.concatenate(qsides, 0)
    qb = np.concatenate([q[::-1] for q in qsides], 0)
    same = [(i // (c >> l)) == (t // (c >> l)) for l in range(levels)] + [i == t]
    return (np.stack([fwd, bwd]).astype(np.float32), np.stack([qf, qb]).astype(np.float32),
            np.stack(same).astype(np.float32))


def _gla_kernel(q_ref, k_ref, v_ref, z_ref, w2_ref, b2_ref, mall_ref, qsel_ref, same_ref, *rest,
                ctx_chunks, n_cast):
    cast_in = rest[:n_cast]
    o_ref = rest[n_cast]
    cast_out = rest[n_cast + 1:2 * n_cast + 1]
    st_ref = rest[2 * n_cast + 1]
    s = pl.program_id(2)
    c = GLA_CHUNK
    levels = same_ref.shape[0] - 1

    for src, dst in zip(cast_in, cast_out):
        dst[...] = src[...].astype(dst.dtype)

    @pl.when(s == 0)
    def _():
        st_ref[...] = jnp.zeros_like(st_ref)

    z_hi, z_lo = _split2(z_ref[0])
    w_hi, w_lo = _split2(w2_ref[0])
    x = _dot(z_hi, w_hi) + _dot(z_hi, w_lo) + _dot(z_lo, w_hi) + b2_ref[0]
    g = (jnp.minimum(x, 0.0) - jnp.log1p(jnp.exp(-jnp.abs(x)))) * (1.0 / GLA_GATE_TEMP)
    mall = mall_ref[0]
    g_hi, g_lo = _split2(g)
    e_all = jnp.exp(_dot(mall, g_hi) + _dot(mall, g_lo))
    qsel = qsel_ref[0]
    outs = []
    for h in range(GLA_HEADS):
        e = e_all[:, h * GLA_DK:(h + 1) * GLA_DK]
        q = q_ref[0, :, h * GLA_DK:(h + 1) * GLA_DK]
        k = k_ref[0, :, h * GLA_DK:(h + 1) * GLA_DK]
        v = v_ref[0, :, h * GLA_DV:(h + 1) * GLA_DV]
        wq = e[:levels * c] * qsel
        wk = e[:levels * c] - wq
        att = _dot_nt(q.astype(BF16), k.astype(BF16)) * same_ref[levels]
        for l in range(levels):
            ql = (q * wq[l * c:(l + 1) * c]).astype(BF16)
            kl = (k * wk[l * c:(l + 1) * c]).astype(BF16)
            att = att + _dot_nt(ql, kl) * same_ref[l]
        e_cum = e[levels * c:(levels + 1) * c]
        e_rest = e[(levels + 1) * c:(levels + 2) * c]
        e_tot = e[(levels + 2) * c:(levels + 2) * c + 1]
        st = st_ref[h]
        outs.append(_dot_nt((q * e_cum).astype(BF16), st.astype(BF16)) + _dot(att.astype(BF16), v))
        st_ref[h] = st * e_tot + _dot_tn(v, (k * e_rest).astype(BF16))

    @pl.when(s >= ctx_chunks)
    def _():
        o_ref[0, 0] = jnp.concatenate(outs, axis=1)


CAST_SLAB_BYTES = 4 * 1024 * 1024


def _cast_slab_rows(rows, width, steps):
    r = 16
    while r <= rows:
        if rows % r == 0 and rows // r <= steps:
            return r if r * width * 4 <= CAST_SLAB_BYTES else None
        r += 16
    return None


def _gla_call(qg, kg, vg, zg, w2p, b2p, n, to_cast=()):
    b = qg.shape[0]
    c = GLA_CHUNK
    total = qg.shape[1] // c
    nl = n // c
    nc = total - nl
    steps = 2 * b * total
    slabs = [_cast_slab_rows(w.shape[0], w.shape[1], steps) for w in to_cast]
    if any(r is None for r in slabs):
        return [_gla_call(qg, kg, vg, zg, w2p, b2p, n)[0]] + [w.astype(BF16) for w in to_cast]

    def slab_spec(w, r):
        last = w.shape[0] // r - 1
        return pl.BlockSpec((r, w.shape[1]), lambda d, bi, s: (jnp.minimum((d * b + bi) * total + s, last), 0))

    cast_specs = [slab_spec(w, r) for w, r in zip(to_cast, slabs)]
    mall, qsel, same = _gla_constants()
    mall = jnp.asarray(mall, BF16)
    qsel = jnp.asarray(qsel, F32)
    same = jnp.asarray(same, F32)

    def chunk(d, s):
        back = jnp.where(s < nc, nc - 1 - s, total - 1 - (s - nc))
        return jnp.where(d == 0, s, back)

    def out_chunk(d, s):
        first = jnp.where(d == 0, 0, nl - 1)
        return jnp.where(s < nc, first, chunk(d, s) - nc)

    return pl.pallas_call(
        functools.partial(_gla_kernel, ctx_chunks=nc, n_cast=len(to_cast)),
        grid=(2, b, total),
        in_specs=[pl.BlockSpec((1, c, GK_WIDTH), lambda d, bi, s: (bi, chunk(d, s), 0)),
                  pl.BlockSpec((1, c, GK_WIDTH), lambda d, bi, s: (bi, chunk(d, s), 0)),
                  pl.BlockSpec((1, c, GLA_WIDTH), lambda d, bi, s: (bi, chunk(d, s), 0)),
                  pl.BlockSpec((1, c, LANES), lambda d, bi, s: (bi, chunk(d, s), 0)),
                  pl.BlockSpec((1, LANES, GK_WIDTH), lambda d, bi, s: (d, 0, 0)),
                  pl.BlockSpec((1, 1, GK_WIDTH), lambda d, bi, s: (d, 0, 0)),
                  pl.BlockSpec((1,) + mall.shape[1:], lambda d, bi, s: (d, 0, 0)),
                  pl.BlockSpec((1,) + qsel.shape[1:], lambda d, bi, s: (d, 0, 0)),
                  pl.BlockSpec(same.shape, lambda d, bi, s: (0, 0, 0))] + cast_specs,
        out_specs=[pl.BlockSpec((1, 1, c, GLA_WIDTH), lambda d, bi, s: (d, bi, out_chunk(d, s), 0))] + cast_specs,
        out_shape=[jax.ShapeDtypeStruct((2, b, n, GLA_WIDTH), F32)]
                  + [jax.ShapeDtypeStruct(w.shape, BF16) for w in to_cast],
        scratch_shapes=[pltpu.VMEM((GLA_HEADS, GLA_DV, GLA_DK), F32)],
        compiler_params=pltpu.CompilerParams(
            dimension_semantics=("arbitrary", "arbitrary", "arbitrary"), vmem_limit_bytes=VMEM_LIMIT),
        name="gla_scan",
    )(qg, kg, vg, zg, w2p, b2p, mall, qsel, same, *to_cast)


def _outproj_kernel(attn_ref, of_ref, ob_ref, r_ref, x_ref, wo_ref, ggla_ref, gta_ref, gffn_ref, modf_ref,
                    wr_ref, br_ref, x1_ref, h2_ref, idx_ref, gate_ref):
    o = of_ref[0, 0] + ob_ref[0, 0]
    r = r_ref[0]
    y = jnp.concatenate([_rms(o[:, h * GLA_DV:(h + 1) * GLA_DV]) for h in range(GLA_HEADS)], axis=1)
    gla = (y * ggla_ref[...] * (r * jax.nn.sigmoid(r))).astype(BF16)
    mix = _dot(attn_ref[0], wo_ref[:ATTN_WIDTH]) + _dot(gla, wo_ref[ATTN_WIDTH:])
    x1 = x_ref[0] + gta_ref[0] * mix
    x1_ref[0] = x1
    h2 = _rms(x1) * gffn_ref[...] * modf_ref[0, 1:2, :] + modf_ref[0, 0:1, :]
    h2_ref[0] = h2
    h_hi, h_lo = _split2(h2)
    w_hi, w_lo = _split2(wr_ref[...])
    vals = _dot_nt(w_hi, h_hi) + _dot_nt(w_hi, h_lo) + _dot_nt(w_lo, h_hi) + br_ref[...]
    ei = lax.broadcasted_iota(jnp.int32, vals.shape, 0)
    tops, ids = [], []
    for _ in range(TOP_K):
        m = jnp.max(vals, axis=0, keepdims=True)
        ix = jnp.min(jnp.where(vals == m, ei, N_EXPERTS), axis=0, keepdims=True)
        tops.append(m)
        ids.append(ix)
        vals = jnp.where(ei == ix, -jnp.inf, vals)
    es = [jnp.exp(m - tops[0]) for m in tops]
    den = es[0] + es[1] + es[2] + es[3]
    gate_ref[...] = jnp.concatenate([e / den for e in es], axis=0)
    idx_ref[...] = jnp.concatenate(ids, axis=0)


def _outproj_call(attn, o_gla, rg, x, w_out, g_gla_t, gt_a, g_ffn, mod_f, w_rt, b_r):
    b, n, d = x.shape
    tm = ROW_TILE
    nt = n // tm
    mo = (rg.shape[1] - n) // tm
    return pl.pallas_call(
        _outproj_kernel,
        grid=(b, nt),
        in_specs=[pl.BlockSpec((1, tm, ATTN_WIDTH), lambda bi, t: (bi, t, 0)),
                  pl.BlockSpec((1, 1, tm, GLA_WIDTH), lambda bi, t: (0, bi, t, 0)),
                  pl.BlockSpec((1, 1, tm, GLA_WIDTH), lambda bi, t: (1, bi, t, 0)),
                  pl.BlockSpec((1, tm, GLA_WIDTH), lambda bi, t: (bi, mo + t, 0)),
                  pl.BlockSpec((1, tm, d), lambda bi, t: (bi, t, 0)),
                  pl.BlockSpec(w_out.shape, lambda bi, t: (0, 0)),
                  pl.BlockSpec((1, GLA_WIDTH), lambda bi, t: (0, 0)),
                  pl.BlockSpec((1, 1, d), lambda bi, t: (bi, 0, 0)),
                  pl.BlockSpec((1, d), lambda bi, t: (0, 0)),
                  pl.BlockSpec((1, 2, d), lambda bi, t: (bi, 0, 0)),
                  pl.BlockSpec((N_EXPERTS, d), lambda bi, t: (0, 0)),
                  pl.BlockSpec((N_EXPERTS, 1), lambda bi, t: (0, 0))],
        out_specs=[pl.BlockSpec((1, tm, d), lambda bi, t: (bi, t, 0)),
                   pl.BlockSpec((1, tm, d), lambda bi, t: (bi, t, 0)),
                   pl.BlockSpec((TOP_K, tm), lambda bi, t: (0, bi * nt + t)),
                   pl.BlockSpec((TOP_K, tm), lambda bi, t: (0, bi * nt + t))],
        out_shape=[jax.ShapeDtypeStruct((b, n, d), F32), jax.ShapeDtypeStruct((b, n, d), F32),
                   jax.ShapeDtypeStruct((TOP_K, b * n), jnp.int32), jax.ShapeDtypeStruct((TOP_K, b * n), F32)],
        compiler_params=pltpu.CompilerParams(dimension_semantics=("arbitrary", "arbitrary"),
                                             vmem_limit_bytes=VMEM_LIMIT),
        name="out_proj_router",
    )(attn, o_gla, o_gla, rg, x, w_out, g_gla_t, gt_a, g_ffn.reshape(1, d), mod_f, w_rt, b_r)


def _route_metadata(top_idx, rows_per_block):
    t = top_idx.shape[1]
    n_assign = t * TOP_K
    tm = rows_per_block
    expert = top_idx.reshape(n_assign)
    _, a_sorted = lax.sort((expert, jnp.arange(n_assign, dtype=jnp.int32)), num_keys=1, is_stable=True)
    counts = jnp.sum((expert[:, None] == jnp.arange(N_EXPERTS, dtype=jnp.int32)[None, :]).astype(jnp.int32), axis=0)
    start = jnp.cumsum(counts) - counts
    padded = (counts + tm - 1) // tm * tm
    pad_end = jnp.cumsum(padded)
    pad_start = pad_end - padded
    n_blocks = n_assign // tm + N_EXPERTS
    blk = jnp.arange(n_blocks, dtype=jnp.int32)
    block_expert = jnp.minimum(jnp.sum((pad_end[None, :] <= (blk * tm)[:, None]).astype(jnp.int32), axis=1),
                               N_EXPERTS - 1)
    offset = blk * tm - pad_start[block_expert]
    block_valid = jnp.clip(counts[block_expert] - offset, 0, tm).astype(jnp.int32)
    first = jnp.clip(start[block_expert] + offset, 0, n_assign)
    r = jnp.arange(tm, dtype=jnp.int32)[None, :]
    a_blk = a_sorted[jnp.minimum(first[:, None] + r, n_assign - 1)]
    valid = r < block_valid[:, None]
    src = (a_blk % t).astype(jnp.int32)
    spill = jnp.broadcast_to(n_assign + r, (1, tm)).astype(jnp.int32)
    dst = jnp.where(valid, a_blk, spill).astype(jnp.int32)
    dst_prev = jnp.concatenate([spill, dst[:-1]], axis=0)
    return (block_expert.astype(jnp.int32), block_valid, src.reshape(n_blocks, 1, tm),
            dst_prev.reshape(n_blocks, 1, tm))


def _moe_kernel(be_ref, bv_ref, src_ref, srcn_ref, dstp_ref, h_hbm,
                wg_ref, wl_ref, bg_ref, bl_ref, wd_ref, bd_ref, y_hbm,
                xbuf, xb, acc, ybuf, gsem, ssem):
    i = pl.program_id(0)
    j = pl.program_id(1)
    tm = xb.shape[0]
    par = i % 2
    nv = bv_ref[i]
    prev_nv = bv_ref[jnp.maximum(i - 1, 0)]
    live = nv > 0
    first_empty = jnp.logical_and(jnp.logical_not(live), jnp.logical_and(i > 0, prev_nv > 0))

    def gather_row(ids_ref, r, to):
        return pltpu.make_async_copy(h_hbm.at[pl.ds(ids_ref[0, 0, r], 1)], xbuf.at[to, pl.ds(r, 1)], gsem.at[to])

    def scatter_row(r, frm):
        return pltpu.make_async_copy(ybuf.at[frm, pl.ds(r, 1)], y_hbm.at[pl.ds(dstp_ref[0, 0, r], 1)], ssem.at[frm])

    def wait_gather(slot):
        pltpu.make_async_copy(h_hbm.at[pl.ds(0, tm)], xbuf.at[slot], gsem.at[slot]).wait()

    def wait_scatter(slot):
        pltpu.make_async_copy(ybuf.at[slot], y_hbm.at[pl.ds(0, tm)], ssem.at[slot]).wait()

    def half_ffn():
        x = xb[...]
        glu = jnp.minimum(_dot(x, wg_ref[0]) + bg_ref[0], SWIGLU_LIMIT)
        lin = jnp.clip(_dot(x, wl_ref[0]) + bl_ref[0], -SWIGLU_LIMIT, SWIGLU_LIMIT)
        act = glu * jax.nn.sigmoid(SWIGLU_ALPHA * glu) * (lin + 1.0)
        return _dot(act.astype(BF16), wd_ref[0])

    @pl.when(jnp.logical_and(jnp.logical_and(i == 0, j == 0), live))
    def _():
        ybuf[1] = jnp.zeros(ybuf.shape[1:], ybuf.dtype)

        def body(r, carry):
            gather_row(src_ref, r, 0).start()
            return carry
        lax.fori_loop(0, tm, body, 0)

    @pl.when(jnp.logical_and(j == 0, jnp.logical_or(live, first_empty)))
    def _():
        wait_gather(par)

    @pl.when(jnp.logical_and(j == 0, live))
    def _():
        xb[...] = xbuf[par].astype(BF16)
        acc[...] = half_ffn()
        for r in range(tm):
            gather_row(srcn_ref, r, 1 - par).start()

    @pl.when(jnp.logical_and(j == 1, live))
    def _():
        part = half_ffn()
        for r in range(tm):
            scatter_row(r, 1 - par).start()

        @pl.when(i >= 1)
        def _():
            wait_scatter(par)

        ybuf[par] = acc[...] + part + bd_ref[0]

    @pl.when(jnp.logical_and(j == 1, first_empty))
    def _():
        def body(r, carry):
            scatter_row(r, 1 - par).start()
            return carry
        lax.fori_loop(0, tm, body, 0)
        wait_scatter(par)
        wait_scatter(1 - par)


def _moe_call(h2, meta, w_gu, b_gu, w_down, b_down):
    t, d = h2.shape
    block_expert, block_valid, src, dst_prev = meta
    nb, _, tm = src.shape
    ff = w_down.shape[1]
    nf = 2
    tf = ff // nf
    grid_spec = pltpu.PrefetchScalarGridSpec(
        num_scalar_prefetch=2,
        grid=(nb, nf),
        in_specs=[pl.BlockSpec((1, 1, tm), lambda i, j, be, bv: (i, 0, 0), memory_space=pltpu.SMEM),
                  pl.BlockSpec((1, 1, tm), lambda i, j, be, bv: (jnp.minimum(i + 1, nb - 1), 0, 0),
                               memory_space=pltpu.SMEM),
                  pl.BlockSpec((1, 1, tm), lambda i, j, be, bv: (i, 0, 0), memory_space=pltpu.SMEM),
                  pl.BlockSpec(memory_space=pl.ANY),
                  pl.BlockSpec((1, d, tf), lambda i, j, be, bv: (be[i], 0, j)),
                  pl.BlockSpec((1, d, tf), lambda i, j, be, bv: (be[i], 0, nf + j)),
                  pl.BlockSpec((1, 1, tf), lambda i, j, be, bv: (be[i], 0, j)),
                  pl.BlockSpec((1, 1, tf), lambda i, j, be, bv: (be[i], 0, nf + j)),
                  pl.BlockSpec((1, tf, d), lambda i, j, be, bv: (be[i], j, 0)),
                  pl.BlockSpec((1, 1, d), lambda i, j, be, bv: (be[i], 0, 0))],
        out_specs=pl.BlockSpec(memory_space=pl.ANY),
        scratch_shapes=[pltpu.VMEM((2, tm, d), F32), pltpu.VMEM((tm, d), BF16), pltpu.VMEM((tm, d), F32),
                        pltpu.VMEM((2, tm, d), F32), pltpu.SemaphoreType.DMA((2,)), pltpu.SemaphoreType.DMA((2,))],
    )
    return pl.pallas_call(
        _moe_kernel,
        grid_spec=grid_spec,
        out_shape=jax.ShapeDtypeStruct((t * TOP_K + tm, d), F32),
        compiler_params=pltpu.CompilerParams(dimension_semantics=("arbitrary", "arbitrary"),
                                             vmem_limit_bytes=VMEM_LIMIT, has_side_effects=True),
        name="moe_experts",
    )(block_expert, block_valid, src, src, dst_prev, h2,
      w_gu, w_gu, b_gu.reshape(N_EXPERTS, 1, 2 * ff), b_gu.reshape(N_EXPERTS, 1, 2 * ff),
      w_down, b_down.reshape(N_EXPERTS, 1, d))


def _combine_kernel(y0_ref, y1_ref, y2_ref, y3_ref, gate_ref, x1_ref, gtf_ref, gfin_ref, o_ref):
    g = gate_ref[...]
    y = ((y0_ref[...] * g[:, 0:1] + y1_ref[...] * g[:, 1:2])
         + (y2_ref[...] * g[:, 2:3] + y3_ref[...] * g[:, 3:4]))
    x2 = x1_ref[0] + gtf_ref[0] * y
    o_ref[0] = _rms(x2) * gfin_ref[...]


def _combine_call(y_rows, gates_t, x1, gt_f, g_final):
    b, n, d = x1.shape
    tm = ROW_TILE
    nt = n // tm
    per_slot = b * nt
    assert TOP_K == 4

    def slot_spec(kk):
        return pl.BlockSpec((tm, d), lambda bi, t: (kk * per_slot + bi * nt + t, 0))

    return pl.pallas_call(
        _combine_kernel,
        grid=(b, nt),
        in_specs=[slot_spec(0), slot_spec(1), slot_spec(2), slot_spec(3),
                  pl.BlockSpec((tm, TOP_K), lambda bi, t: (bi * nt + t, 0)),
                  pl.BlockSpec((1, tm, d), lambda bi, t: (bi, t, 0)),
                  pl.BlockSpec((1, 1, d), lambda bi, t: (bi, 0, 0)),
                  pl.BlockSpec((1, d), lambda bi, t: (0, 0))],
        out_specs=pl.BlockSpec((1, tm, d), lambda bi, t: (bi, t, 0)),
        out_shape=jax.ShapeDtypeStruct((b, n, d), F32),
        compiler_params=pltpu.CompilerParams(dimension_semantics=("arbitrary", "arbitrary"),
                                             vmem_limit_bytes=VMEM_LIMIT),
        name="combine_norm",
    )(y_rows, y_rows, y_rows, y_rows, gates_t, x1, gt_f, g_final.reshape(1, d))


def _rope_tables(n, m):
    pos = jnp.arange(n, dtype=jnp.int32)
    row = (pos // GRID_W).astype(F32)
    col = (pos % GRID_W).astype(F32)
    n_freq = HEAD_DIM // 4
    inv_freq = ROPE_BASE ** (-jnp.arange(n_freq, dtype=F32) / n_freq)
    ang_r = row[:, None] * inv_freq
    ang_c = col[:, None] * inv_freq
    cos = jnp.concatenate([jnp.cos(ang_r), jnp.cos(ang_r), jnp.cos(ang_c), jnp.cos(ang_c)], axis=1)
    sin = jnp.concatenate([-jnp.sin(ang_r), jnp.sin(ang_r), -jnp.sin(ang_c), jnp.sin(ang_c)], axis=1)
    cos = jnp.tile(cos, (1, LANES // HEAD_DIM))
    sin = jnp.tile(sin, (1, LANES // HEAD_DIM))
    cos = jnp.concatenate([jnp.ones((m, LANES), F32), cos], axis=0)
    sin = jnp.concatenate([jnp.zeros((m, LANES), F32), sin], axis=0)
    return cos, sin


def _layer(x, c, ctx, c_ctx, w_ada, b_ada, g_mix, g_ffn, w_in, attn_sink, w_gate2, b_gate2, g_gla,
           w_out, w_router, b_router, w_gu, b_gu, w_down, b_down, g_final):
    b, n, d = x.shape
    m = ctx.shape[1]

    cc = jnp.concatenate([c, c_ctx[None, :], jnp.zeros((8 - b - 1, d), F32)], axis=0)
    mod = _ada_call(cc, w_ada, b_ada)
    sh_a, sc_a, gt_a, sh_f, sc_f, gt_f = [mod[:, i * d:(i + 1) * d] for i in range(6)]
    lat = jnp.stack([sh_a[:b], 1.0 + sc_a[:b]], axis=1)
    cx = jnp.broadcast_to(jnp.stack([sh_a[b], 1.0 + sc_a[b]], axis=0)[None], (b, 2, d))
    mod_a = jnp.stack([cx, lat], axis=1)
    mod_f = jnp.stack([sh_f[:b], 1.0 + sc_f[:b]], axis=1)

    w_main = w_in[:, :MAIN_WIDTH].astype(BF16)
    w_z = jnp.pad(w_in[:, MAIN_WIDTH:], ((0, 0), (0, LANES - 2 * GLA_GATE_RANK))).astype(BF16)
    cos_t, sin_t = _rope_tables(n, m)
    qa, ka, va, qg, kg, vg, rg, zg = _inproj_call(ctx, x, mod_a, g_mix, cos_t, sin_t, w_main, w_z)

    attn = _attn_call(attn_sink, qa, ka, va, n)

    w2p = jnp.zeros((2, LANES, GK_WIDTH), F32)
    for dd in range(2):
        w2p = w2p.at[dd, dd * GLA_GATE_RANK:(dd + 1) * GLA_GATE_RANK].set(w_gate2[dd])
    ne, _, ff2 = w_gu.shape
    ff = w_down.shape[1]
    o_gla, w_gu_b, w_down_b = _gla_call(qg, kg, vg, zg, w2p, b_gate2.reshape(2, 1, GK_WIDTH), n,
                                        to_cast=(w_gu.reshape(ne * d, ff2), w_down.reshape(ne * ff, d)))
    w_gu_b = w_gu_b.reshape(ne, d, ff2)
    w_down_b = w_down_b.reshape(ne, ff, d)

    x1, h2, top_idx, gates = _outproj_call(
        attn, o_gla, rg, x, w_out.astype(BF16), jnp.tile(g_gla, GLA_HEADS).reshape(1, GLA_WIDTH),
        gt_a[:b, None, :], g_ffn, mod_f, w_router.T, b_router.reshape(N_EXPERTS, 1))

    meta = _route_metadata(top_idx, MOE_ROWS)
    y_rows = _moe_call(h2.reshape(b * n, d), meta, w_gu_b, b_gu, w_down_b, b_down)
    return _combine_call(y_rows, gates.T, x1, gt_f[:b, None, :], g_final)


def kernel(x, c, ctx, c_ctx, w_ada, b_ada, g_mix, g_ffn, w_in, attn_sink, w_gate2, b_gate2, g_gla, w_out,
           w_router, b_router, w_gu, b_gu, w_down, b_down, g_final):
    assert w_ada.shape[0] == 1, "single-layer problem: the context stream is never updated"
    return _layer(x, c, ctx, c_ctx, w_ada[0], b_ada[0], g_mix[0], g_ffn[0], w_in[0], attn_sink[0], w_gate2[0],
                  b_gate2[0], g_gla[0], w_out[0], w_router[0], b_router[0], w_gu[0], b_gu[0], w_down[0],
                  b_down[0], g_final)
```
